```python
import jax
import jax.numpy as jnp
from jax import lax
import numpy as np

D_MODEL = 2048
BATCH = 32
SEQ = 256
DEPTH = 4
DEC_BATCH = 8
DEC_SEQ = 2048
PAST_LEN = 512

GRID_W = 64
N_MIXERS = 2
N_RET = (DEPTH + 1) // 2
N_CMLP = DEPTH // 2
RET_HEADS = 8
RET_DK = D_MODEL // RET_HEADS
RET_DV = 2 * RET_DK
RET_CHUNK = 128
ROPE_BASE = 10000.0
CM_CHUNK = 128
CM_GROUPS = 8
CM_WIDTH = 3 * D_MODEL
N_GROUPS = 4
EXPERTS_PER_GROUP = 4
N_EXPERTS = N_GROUPS * EXPERTS_PER_GROUP
EXPERT_HIDDEN = D_MODEL // 2
TOP_K = 2
MOE_BLOCK = 256
NORM_EPS = 1e-6

kernel_name = 'hybrid_retention_gmlp_hmoe_diffusion_step'


def rmsnorm(x, g):
    x32 = x.astype(jnp.float32)
    y = x32 * lax.rsqrt(jnp.mean(x32 * x32, axis=-1, keepdims=True) + NORM_EPS)
    return (y * g.astype(jnp.float32)).astype(x.dtype)


def adaln(cond, w, b):
    m = (jax.nn.silu(cond) @ w + b)[:, None, :]
    return jnp.split(m, 6, axis=-1)


def grid_rope(x):
    l = x.shape[2]
    rows = l // GRID_W
    row = jnp.broadcast_to(jnp.arange(rows)[:, None], (rows, GRID_W)).reshape(-1).astype(jnp.float32)
    col = jnp.broadcast_to(jnp.arange(GRID_W)[None, :], (rows, GRID_W)).reshape(-1).astype(jnp.float32)
    n_freq = x.shape[-1] // 4
    inv = ROPE_BASE ** (-jnp.arange(n_freq, dtype=jnp.float32) / n_freq)
    ang = jnp.concatenate([row[:, None] * inv, col[:, None] * inv], axis=-1)
    cos, sin = jnp.cos(ang), jnp.sin(ang)
    x1, x2 = jnp.split(x, 2, axis=-1)
    return jnp.concatenate([x1 * cos - x2 * sin, x1 * sin + x2 * cos], axis=-1)


def retention_scan(q, k, v, log_g, s0):
    b, h, l, _ = q.shape
    dv = v.shape[-1]
    n = l // RET_CHUNK
    idx = jnp.arange(RET_CHUNK, dtype=jnp.float32)
    rel = idx[:, None] - idx[None, :]
    intra = jnp.where(rel >= 0, jnp.exp(log_g[:, None, None] * jnp.maximum(rel, 0.0)), 0.0)
    q_decay = jnp.exp(log_g[:, None] * (idx + 1.0))[..., None]
    k_decay = jnp.exp(log_g[:, None] * (RET_CHUNK - 1.0 - idx))[..., None]
    chunk_decay = jnp.exp(log_g * RET_CHUNK)[:, None, None]

    def to_chunks(t):
        return t.reshape(b, h, n, RET_CHUNK, t.shape[-1]).transpose(2, 0, 1, 3, 4)

    def step(s, qkv):
        qc, kc, vc = qkv
        scores = jnp.einsum('bhid,bhjd->bhij', qc, kc) * intra
        o = (jnp.einsum('bhij,bhje->bhie', scores, vc)
             + jnp.einsum('bhid,bhde->bhie', qc * q_decay, s))
        s = chunk_decay * s + jnp.einsum('bhjd,bhje->bhde', kc * k_decay, vc)
        return s, o

    s_fin, o = lax.scan(step, s0, (to_chunks(q), to_chunks(k), to_chunks(v)))
    return o.transpose(1, 2, 0, 3, 4).reshape(b, h, l, dv), s_fin


def retention_mixer(h, w_qkvg, w_o, gn_g, log_decay, s_init, use_grid):
    b, l, _ = h.shape
    hk, hv = RET_HEADS * RET_DK, RET_HEADS * RET_DV
    q, k, v, g = jnp.split(h @ w_qkvg, [hk, 2 * hk, 2 * hk + hv], axis=-1)
    heads = lambda t, d: t.reshape(b, l, RET_HEADS, d).transpose(0, 2, 1, 3).astype(jnp.float32)
    q, k, v = heads(q, RET_DK), heads(k, RET_DK), heads(v, RET_DV)
    if use_grid:
        q, k = grid_rope(q), grid_rope(k)
    k = k * (RET_DK ** -0.5)
    log_g = -jnp.exp(log_decay.astype(jnp.float32))
    if s_init is None:
        s_init = jnp.zeros((b, 2, RET_HEADS, RET_DK, RET_DV), jnp.float32)
    s_init = s_init.astype(jnp.float32)
    o_f, s_f = retention_scan(q, k, v, log_g[0], s_init[:, 0])
    rev = lambda t: jnp.flip(t, axis=2)
    o_b, s_b = retention_scan(rev(q), rev(k), rev(v), log_g[1], s_init[:, 1])
    o = o_f + rev(o_b)
    mu = jnp.mean(o, axis=-1, keepdims=True)
    var = jnp.mean(jnp.square(o - mu), axis=-1, keepdims=True)
    o = ((o - mu) * lax.rsqrt(var + NORM_EPS)).transpose(0, 2, 1, 3).reshape(b, l, hv)
    o = (o * gn_g.astype(jnp.float32)).astype(h.dtype) * jax.nn.silu(g)
    return o @ w_o, jnp.stack([s_f, s_b], axis=1)


def chunk_mlp_mixer(h, w_in, ln_g, ln_b, w_s, b_s, w_out):
    b, l, _ = h.shape
    u, v = jnp.split(jax.nn.gelu(h @ w_in), 2, axis=-1)
    v32 = v.astype(jnp.float32)
    mu = jnp.mean(v32, axis=-1, keepdims=True)
    var = jnp.mean(jnp.square(v32 - mu), axis=-1, keepdims=True)
    v = ((v32 - mu) * lax.rsqrt(var + NORM_EPS) * ln_g.astype(jnp.float32)
         + ln_b.astype(jnp.float32)).astype(h.dtype)
    vb = v.reshape(b, l // CM_CHUNK, CM_CHUNK, CM_GROUPS, CM_WIDTH // CM_GROUPS)
    s = jnp.einsum('gpq,bnqgc->bnpgc', w_s, vb) + b_s.T[None, None, :, :, None]
    return (u * s.reshape(b, l, CM_WIDTH)) @ w_out


def moe_ffn(h, w_group, b_group, w_router, b_router, w_gate, w_up, w_down):
    b, l, d = h.shape
    n = b * l
    a = n * TOP_K
    x = h.reshape(n, d)
    group_logits = (x @ w_group).astype(jnp.float32) + b_group.astype(jnp.float32)
    grp = jnp.argmax(group_logits, axis=-1)
    p_grp = jnp.take_along_axis(jax.nn.softmax(group_logits, axis=-1), grp[:, None], axis=-1)
    expert_logits = ((x @ w_router).astype(jnp.float32) + b_router.astype(jnp.float32)).reshape(
        n, N_GROUPS, EXPERTS_PER_GROUP)
    in_group = jnp.take_along_axis(expert_logits, grp[:, None, None], axis=1)[:, 0]
    top_v, top_i = lax.top_k(in_group, TOP_K)
    weights = (jax.nn.softmax(top_v, axis=-1) * p_grp).reshape(-1)
    expert_ids = (grp[:, None] * EXPERTS_PER_GROUP + top_i).reshape(-1)
    tokens = jnp.arange(a) // TOP_K
    order = jnp.argsort(expert_ids)
    e_s, t_s, w_s = expert_ids[order], tokens[order], weights[order]
    counts = jnp.bincount(expert_ids, length=N_EXPERTS)
    starts = jnp.cumsum(counts) - counts
    padded = (counts + MOE_BLOCK - 1) // MOE_BLOCK * MOE_BLOCK
    pad_ends = jnp.cumsum(padded)
    pos = pad_ends[e_s] - padded[e_s] + jnp.arange(a) - starts[e_s]
    n_blocks = -(-a // MOE_BLOCK) + N_EXPERTS
    xb = jnp.zeros((n_blocks * MOE_BLOCK, d), x.dtype).at[pos].set(x[t_s])
    block_e = jnp.minimum(jnp.searchsorted(pad_ends, jnp.arange(n_blocks) * MOE_BLOCK, side='right'),
                          N_EXPERTS - 1)

    def expert_block(args):
        xblk, e = args
        return (jax.nn.silu(xblk @ w_gate[e]) * (xblk @ w_up[e])) @ w_down[e]

    yb = lax.map(expert_block, (xb.reshape(n_blocks, MOE_BLOCK, d), block_e)).reshape(-1, d)
    y = jnp.zeros((n, d), jnp.float32).at[t_s].add(w_s[:, None] * yb[pos].astype(jnp.float32))
    return y.reshape(b, l, d).astype(h.dtype)


def setup_inputs(seed: int = 0) -> dict:
    key = jax.random.key(seed)
    ks = iter(jax.random.split(key, 32))
    nrm = lambda shape, scale: scale * jax.random.normal(next(ks), shape, jnp.float32)
    D = D_MODEL
    hk, hv = RET_HEADS * RET_DK, RET_HEADS * RET_DV
    base_decay = -(5.0 + jnp.arange(RET_HEADS, dtype=jnp.float32)) * np.float32(np.log(2.0))
    return {
        'x_prompt': nrm((BATCH, SEQ, D), 1.0),
        'x_sample': nrm((DEC_BATCH, DEC_SEQ, D), 1.0),
        'c': nrm((DEC_BATCH, D), 1.0),
        'state_ret': nrm((DEC_BATCH, N_RET, 2, RET_HEADS, RET_DK, RET_DV), 0.5),
        'c_ctx': nrm((D,), 1.0),
        'ada_w': nrm((DEPTH, D, 6 * D), 0.5 * D ** -0.5),
        'ada_b': nrm((DEPTH, 6 * D), 0.02),
        'norm1_g': 1.0 + nrm((DEPTH, D), 0.05),
        'norm2_g': 1.0 + nrm((DEPTH, D), 0.05),
        'ret_w_qkvg': nrm((N_RET, D, 2 * hk + 2 * hv), D ** -0.5),
        'ret_w_o': nrm((N_RET, hv, D), hv ** -0.5),
        'ret_gn_g': 1.0 + nrm((N_RET, hv), 0.05),
        'ret_log_decay': base_decay[None, None, :] + nrm((N_RET, 2, RET_HEADS), 0.05),
        'cm_w_in': nrm((N_CMLP, D, 2 * CM_WIDTH), D ** -0.5),
        'cm_ln_g': 1.0 + nrm((N_CMLP, CM_WIDTH), 0.05),
        'cm_ln_b': nrm((N_CMLP, CM_WIDTH), 0.02),
        'cm_w_s': nrm((N_CMLP, CM_GROUPS, CM_CHUNK, CM_CHUNK), CM_CHUNK ** -0.5),
        'cm_b_s': 1.0 + nrm((N_CMLP, CM_GROUPS, CM_CHUNK), 0.1),
        'cm_w_out': nrm((N_CMLP, CM_WIDTH, D), CM_WIDTH ** -0.5),
        'moe_w_group': nrm((DEPTH, D, N_GROUPS), D ** -0.5),
        'moe_b_group': nrm((DEPTH, N_GROUPS), 0.01),
        'moe_w_router': nrm((DEPTH, D, N_EXPERTS), D ** -0.5),
        'moe_b_router': nrm((DEPTH, N_EXPERTS), 0.01),
        'moe_w_gate': nrm((DEPTH, N_EXPERTS, D, EXPERT_HIDDEN), D ** -0.5),
        'moe_w_up': nrm((DEPTH, N_EXPERTS, D, EXPERT_HIDDEN), D ** -0.5),
        'moe_w_down': nrm((DEPTH, N_EXPERTS, EXPERT_HIDDEN, D), EXPERT_HIDDEN ** -0.5),
        'final_norm_g': 1.0 + nrm((D,), 0.05),
    }


def reference(x_prompt, x_sample, c, state_ret, c_ctx, ada_w, ada_b, norm1_g, norm2_g,
              ret_w_qkvg, ret_w_o, ret_gn_g, ret_log_decay, cm_w_in, cm_ln_g, cm_ln_b, cm_w_s,
              cm_b_s, cm_w_out, moe_w_group, moe_b_group, moe_w_router, moe_b_router,
              moe_w_gate, moe_w_up, moe_w_down, final_norm_g):
    ctx = x_prompt
    lat = x_sample
    ctx_states = []
    for layer in range(DEPTH):
        sh1_c, sc1_c, g1_c, sh2_c, sc2_c, g2_c = adaln(c_ctx[None, :], ada_w[layer], ada_b[layer])
        sh1_x, sc1_x, g1_x, sh2_x, sc2_x, g2_x = adaln(c, ada_w[layer], ada_b[layer])
        hc = rmsnorm(ctx, norm1_g[layer]) * (1.0 + sc1_c) + sh1_c
        hx = rmsnorm(lat, norm1_g[layer]) * (1.0 + sc1_x) + sh1_x
        j = layer // N_MIXERS
        if layer % N_MIXERS == 0:
            oc, s_ctx = retention_mixer(hc, ret_w_qkvg[j], ret_w_o[j], ret_gn_g[j], ret_log_decay[j],
                                        None, False)
            ox, _ = retention_mixer(hx, ret_w_qkvg[j], ret_w_o[j], ret_gn_g[j], ret_log_decay[j],
                                    state_ret[:, j], True)
            ctx_states.append(s_ctx)
        else:
            oc = chunk_mlp_mixer(hc, cm_w_in[j], cm_ln_g[j], cm_ln_b[j], cm_w_s[j], cm_b_s[j], cm_w_out[j])
            ox = chunk_mlp_mixer(hx, cm_w_in[j], cm_ln_g[j], cm_ln_b[j], cm_w_s[j], cm_b_s[j], cm_w_out[j])
        ctx = ctx + g1_c * oc
        lat = lat + g1_x * ox
        moe_args = (moe_w_group[layer], moe_b_group[layer], moe_w_router[layer], moe_b_router[layer],
                    moe_w_gate[layer], moe_w_up[layer], moe_w_down[layer])
        ctx = ctx + g2_c * moe_ffn(rmsnorm(ctx, norm2_g[layer]) * (1.0 + sc2_c) + sh2_c, *moe_args)
        lat = lat + g2_x * moe_ffn(rmsnorm(lat, norm2_g[layer]) * (1.0 + sc2_x) + sh2_x, *moe_args)
    y_prompt = rmsnorm(ctx, final_norm_g)
    y_sample = rmsnorm(lat, final_norm_g)
    new_state_ret = jnp.stack(ctx_states, axis=1)
    return (y_prompt, y_sample, new_state_ret)
```

```python
import functools

import jax
import jax.numpy as jnp
from jax import lax
from jax.experimental import pallas as pl
from jax.experimental.pallas import tpu as pltpu

F32 = jnp.float32
BF16 = jnp.bfloat16

RET_HEADS = 8
GRID_W = 64
ROPE_BASE = 10000.0
CM_CHUNK = 128
CM_GROUPS = 8
N_GROUPS = 4
EXPERTS_PER_GROUP = 4
N_EXPERTS = N_GROUPS * EXPERTS_PER_GROUP
TOP_K = 2
NORM_EPS = 1e-6

MOE_ROWS = 256
ROUTER_LANES = 128
COND_ROWS = 16
RET_CHUNK_ROWS = 256
V7X_VMEM_LIMIT = 56 * 1024 * 1024


def _params(semantics):
    return pltpu.CompilerParams(dimension_semantics=semantics, vmem_limit_bytes=V7X_VMEM_LIMIT)


def _tile(n, pref):
    t = min(n, pref)
    while n % t:
        t //= 2
    return t


def _ada_kernel(cond_ref, w_ref, b_ref, o_ref):
    s = jax.nn.silu(cond_ref[...])
    o_ref[0] = jnp.dot(s.astype(BF16), w_ref[0].astype(BF16), preferred_element_type=F32) + b_ref[0]


def _ada_modulation(cond, ada_w, ada_b):
    depth, d, n = ada_w.shape
    tn = _tile(n, 1024)
    return pl.pallas_call(
        _ada_kernel,
        grid=(depth, n // tn),
        in_specs=[
            pl.BlockSpec((COND_ROWS, d), lambda l, j: (0, 0)),
            pl.BlockSpec((1, d, tn), lambda l, j: (l, 0, j)),
            pl.BlockSpec((1, 1, tn), lambda l, j: (l, 0, j)),
        ],
        out_specs=pl.BlockSpec((1, COND_ROWS, tn), lambda l, j: (l, 0, j)),
        out_shape=jax.ShapeDtypeStruct((depth, COND_ROWS, n), F32),
        compiler_params=_params(("parallel", "parallel")),
        name="ada_modulation",
    )(cond, ada_w, ada_b.reshape(depth, 1, n))


def _norm_mod(x, gamma, shift, scale):
    y = x * lax.rsqrt(jnp.mean(x * x, axis=-1, keepdims=True) + NORM_EPS)
    return (y * gamma) * (1.0 + scale) + shift


def _mod_row(tile_start, t_ctx, l_lat):
    return jnp.where(tile_start < t_ctx, 0, 1 + jnp.maximum(tile_start - t_ctx, 0) // l_lat)


def _norm_matmul_kernel(x_ref, g_ref, mod_ref, w_ref, o_ref, h_ref, *, shift_idx, scale_idx, act):
    @pl.when(pl.program_id(1) == 0)
    def _():
        m = mod_ref[0]
        h = _norm_mod(x_ref[...], g_ref[...], m[shift_idx:shift_idx + 1], m[scale_idx:scale_idx + 1])
        h_ref[...] = h.astype(BF16)

    acc = jnp.dot(h_ref[...], w_ref[...], preferred_element_type=F32)
    if act == "gelu":
        acc = jax.nn.gelu(acc)
    o_ref[...] = acc.astype(o_ref.dtype)


def _norm_matmul(x, gamma, mod, w, *, shift_idx, scale_idx, act, t_ctx, l_lat):
    t, d = x.shape
    n = w.shape[1]
    tm = _tile(min(t_ctx, l_lat), 1024)
    tn = _tile(n, 1024)
    return pl.pallas_call(
        functools.partial(_norm_matmul_kernel, shift_idx=shift_idx, scale_idx=scale_idx, act=act),
        grid=(t // tm, n // tn),
        in_specs=[
            pl.BlockSpec((tm, d), lambda i, j: (i, 0)),
            pl.BlockSpec((1, d), lambda i, j: (0, 0)),
            pl.BlockSpec((1, 6, d), lambda i, j: (_mod_row(i * tm, t_ctx, l_lat), 0, 0)),
            pl.BlockSpec((d, tn), lambda i, j: (0, j)),
        ],
        out_specs=pl.BlockSpec((tm, tn), lambda i, j: (i, j)),
        out_shape=jax.ShapeDtypeStruct((t, n), BF16),
        scratch_shapes=[pltpu.VMEM((tm, d), BF16)],
        compiler_params=_params(("parallel", "arbitrary")),
        name="norm_matmul_" + str(act),
    )(x, gamma.reshape(1, d), mod, w)


def _retention_kernel(*refs, chunk, n_chunks, use_rope, has_init, emit_state, dk):
    it = iter(refs)
    lg_ref, q_ref, k_ref, v_ref, g_ref, gn_ref = (next(it) for _ in range(6))
    cos_ref = sin_ref = s0_ref = st_ref = None
    if use_rope:
        cos_ref, sin_ref = next(it), next(it)
    if has_init:
        s0_ref = next(it)
    o_ref = next(it)
    if emit_state:
        st_ref = next(it)
    oacc_ref, s_ref = next(it), next(it)

    head = pl.program_id(1)
    lg_f = lg_ref[0, head]
    lg_b = lg_ref[1, head]
    c = chunk
    pos = lax.broadcasted_iota(jnp.int32, (c, 1), 0).astype(F32)
    qd_f = jnp.exp(lg_f * (pos + 1.0))
    kd_f = jnp.exp(lg_f * (c - 1.0 - pos))
    qd_b = jnp.exp(lg_b * (c - pos))
    kd_b = jnp.exp(lg_b * pos)
    cd_f = jnp.exp(jnp.full((1, 1), c, F32) * lg_f)
    cd_b = jnp.exp(jnp.full((1, 1), c, F32) * lg_b)
    rel = (lax.broadcasted_iota(jnp.int32, (c, c), 0) - lax.broadcasted_iota(jnp.int32, (c, c), 1)).astype(F32)
    mask = (jnp.where(rel >= 0, jnp.exp(lg_f * jnp.maximum(rel, 0.0)), 0.0)
            + jnp.where(rel <= 0, jnp.exp(lg_b * jnp.maximum(-rel, 0.0)), 0.0))
    half = dk // 2
    k_scale = dk ** -0.5

    def rows(ci):
        if isinstance(ci, int):
            return pl.ds(ci * c, c)
        return pl.ds(pl.multiple_of(ci * c, c), c)

    def load_qk(sl):
        q = q_ref[0, sl, :].astype(F32)
        k = k_ref[0, sl, :].astype(F32)
        if use_rope:
            cos = cos_ref[sl, :]
            sin = sin_ref[sl, :]

            def rope(x):
                x1, x2 = x[:, :half], x[:, half:]
                return jnp.concatenate([x1 * cos - x2 * sin, x1 * sin + x2 * cos], axis=-1)

            q, k = rope(q), rope(k)
        return q, k * k_scale

    def state_update(k, kd, cd, v):
        kv = lax.dot_general((k * kd).astype(BF16), v, (((0,), (0,)), ((), ())), preferred_element_type=F32)
        return cd * s_ref[...] + kv

    def finish(sl, o):
        mu = jnp.mean(o, axis=-1, keepdims=True)
        dlt = o - mu
        var = jnp.mean(dlt * dlt, axis=-1, keepdims=True)
        on = dlt * lax.rsqrt(var + NORM_EPS)
        gate = g_ref[0, sl, :].astype(F32)
        o_ref[0, sl, :] = ((on * gn_ref[0]) * jax.nn.silu(gate)).astype(o_ref.dtype)

    def intra(q, k, v):
        a = lax.dot_general(q.astype(BF16), k.astype(BF16), (((1,), (1,)), ((), ())),
                            preferred_element_type=F32) * mask
        return jnp.dot(a.astype(BF16), v, preferred_element_type=F32)

    if n_chunks == 1 and not has_init:
        sl = rows(0)
        q, k = load_qk(sl)
        v = v_ref[0, sl, :]
        finish(sl, intra(q, k, v))
        if emit_state:
            s_ref[...] = jnp.zeros_like(s_ref)
            st_ref[0, 0, 0] = state_update(k, kd_f, cd_f, v)
            st_ref[0, 1, 0] = state_update(k, kd_b, cd_b, v)
        return

    if has_init:
        s_ref[...] = s0_ref[0, 0, 1, 0]
    else:
        s_ref[...] = jnp.zeros_like(s_ref)

    def bwd_body(t, carry):
        sl = rows(n_chunks - 1 - t)
        q, k = load_qk(sl)
        v = v_ref[0, sl, :]
        oacc_ref[sl, :] = jnp.dot((q * qd_b).astype(BF16), s_ref[...].astype(BF16), preferred_element_type=F32)
        s_ref[...] = state_update(k, kd_b, cd_b, v)
        return carry

    lax.fori_loop(0, n_chunks, bwd_body, 0)
    if emit_state:
        st_ref[0, 1, 0] = s_ref[...]

    if has_init:
        s_ref[...] = s0_ref[0, 0, 0, 0]
    else:
        s_ref[...] = jnp.zeros_like(s_ref)

    def fwd_body(ci, carry):
        sl = rows(ci)
        q, k = load_qk(sl)
        v = v_ref[0, sl, :]
        o = (intra(q, k, v)
             + jnp.dot((q * qd_f).astype(BF16), s_ref[...].astype(BF16), preferred_element_type=F32)
             + oacc_ref[sl, :])
        finish(sl, o)
        s_ref[...] = state_update(k, kd_f, cd_f, v)
        return carry

    lax.fori_loop(0, n_chunks, fwd_body, 0)
    if emit_state:
        st_ref[0, 0, 0] = s_ref[...]


def _retention(qkvg, log_g, gn_g, *, seq_len, n_seq, seq_off, rope_tables, init_state, emit_state):
    t, width = qkvg.shape
    h = RET_HEADS
    dk = width // (6 * h)
    dv = 2 * dk
    chunk = min(RET_CHUNK_ROWS, seq_len)
    n_chunks = seq_len // chunk
    x3 = qkvg.reshape(t // seq_len, seq_len, width)
    k_blk, v_blk, g_blk = h, (2 * h * dk) // dv, (2 * h * dk + h * dv) // dv
    use_rope = rope_tables is not None
    has_init = init_state is not None

    in_specs = [
        pl.BlockSpec(memory_space=pltpu.SMEM),
        pl.BlockSpec((1, seq_len, dk), lambda b, hh: (b + seq_off, 0, hh)),
        pl.BlockSpec((1, seq_len, dk), lambda b, hh: (b + seq_off, 0, k_blk + hh)),
        pl.BlockSpec((1, seq_len, dv), lambda b, hh: (b + seq_off, 0, v_blk + hh)),
        pl.BlockSpec((1, seq_len, dv), lambda b, hh: (b + seq_off, 0, g_blk + hh)),
        pl.BlockSpec((1, 1, dv), lambda b, hh: (hh, 0, 0)),
    ]
    args = [log_g, x3, x3, x3, x3, gn_g.reshape(h, 1, dv)]
    if use_rope:
        in_specs += [pl.BlockSpec((seq_len, dk // 2), lambda b, hh: (0, 0))] * 2
        args += list(rope_tables)
    if has_init:
        layer = init_state[1]
        in_specs.append(pl.BlockSpec((1, 1, 2, 1, dk, dv), lambda b, hh: (b, layer, 0, hh, 0, 0)))
        args.append(init_state[0])
    out_specs = [pl.BlockSpec((1, seq_len, dv), lambda b, hh: (b, 0, hh))]
    out_shape = [jax.ShapeDtypeStruct((n_seq, seq_len, h * dv), BF16)]
    if emit_state:
        out_specs.append(pl.BlockSpec((1, 2, 1, dk, dv), lambda b, hh: (b, 0, hh, 0, 0)))
        out_shape.append(jax.ShapeDtypeStruct((n_seq, 2, h, dk, dv), F32))
    outs = pl.pallas_call(
        functools.partial(_retention_kernel, chunk=chunk, n_chunks=n_chunks, use_rope=use_rope,
                          has_init=has_init, emit_state=emit_state, dk=dk),
        grid=(n_seq, h),
        in_specs=in_specs,
        out_specs=out_specs,
        out_shape=out_shape,
        scratch_shapes=[pltpu.VMEM((seq_len, dv), F32), pltpu.VMEM((dk, dv), F32)],
        compiler_params=_params(("parallel", "parallel")),
        name="retention_rope" if use_rope else "retention_ctx",
    )(*args)
    return outs if emit_state else outs[0]


def _rope_tables(seq_len, dk):
    rows = seq_len // GRID_W
    row = jnp.broadcast_to(jnp.arange(rows)[:, None], (rows, GRID_W)).reshape(-1).astype(F32)
    col = jnp.broadcast_to(jnp.arange(GRID_W)[None, :], (rows, GRID_W)).reshape(-1).astype(F32)
    n_freq = dk // 4
    inv = ROPE_BASE ** (-jnp.arange(n_freq, dtype=F32) / n_freq)
    ang = jnp.concatenate([row[:, None] * inv, col[:, None] * inv], axis=-1)
    return jnp.cos(ang), jnp.sin(ang)


def _proj_residual_kernel(a_ref, w_ref, x_ref, mod_ref, o_ref, *, gate_idx):
    acc = jnp.dot(a_ref[...], w_ref[...], preferred_element_type=F32)
    gate = mod_ref[0][gate_idx:gate_idx + 1]
    o_ref[...] = x_ref[...] + gate * acc


def _proj_residual(a, w, x, mod, *, gate_idx, t_ctx, l_lat):
    t, k = a.shape
    d = w.shape[1]
    tm = _tile(min(t_ctx, l_lat), 512)
    tn = _tile(d, 1024)
    return pl.pallas_call(
        functools.partial(_proj_residual_kernel, gate_idx=gate_idx),
        grid=(d // tn, t // tm),
        in_specs=[
            pl.BlockSpec((tm, k), lambda j, i: (i, 0)),
            pl.BlockSpec((k, tn), lambda j, i: (0, j)),
            pl.BlockSpec((tm, tn), lambda j, i: (i, j)),
            pl.BlockSpec((1, 6, tn), lambda j, i: (_mod_row(i * tm, t_ctx, l_lat), 0, j)),
        ],
        out_specs=pl.BlockSpec((tm, tn), lambda j, i: (i, j)),
        out_shape=jax.ShapeDtypeStruct((t, d), F32),
        compiler_params=_params(("parallel", "parallel")),
        name="proj_residual",
    )(a, w, x, mod)


def _spatial_gate_kernel(u_ref, v_ref, lng_ref, lnb_ref, ws_ref, bs_ref, t_ref, vn_ref, *, n_sub, cw):
    v32 = v_ref[...].astype(F32)
    mu = jnp.mean(v32, axis=-1, keepdims=True)
    dlt = v32 - mu
    var = jnp.mean(dlt * dlt, axis=-1, keepdims=True)
    vn_ref[...] = (dlt * lax.rsqrt(var + NORM_EPS) * lng_ref[...] + lnb_ref[...]).astype(BF16)
    for n in range(n_sub):
        rs = slice(n * CM_CHUNK, (n + 1) * CM_CHUNK)
        for g in range(CM_GROUPS):
            cs = slice(g * cw, (g + 1) * cw)
            s = jnp.dot(ws_ref[g], vn_ref[rs, cs], preferred_element_type=F32) + bs_ref[g]
            t_ref[rs, cs] = (u_ref[rs, cs].astype(F32) * s).astype(t_ref.dtype)


def _spatial_gate(uv, ln_g, ln_b, w_s, b_s):
    t, two_cw = uv.shape
    width = two_cw // 2
    cw = width // CM_GROUPS
    tm = 2 * CM_CHUNK
    return pl.pallas_call(
        functools.partial(_spatial_gate_kernel, n_sub=tm // CM_CHUNK, cw=cw),
        grid=(t // tm,),
        in_specs=[
            pl.BlockSpec((tm, width), lambda i: (i, 0)),
            pl.BlockSpec((tm, width), lambda i: (i, 1)),
            pl.BlockSpec((1, width), lambda i: (0, 0)),
            pl.BlockSpec((1, width), lambda i: (0, 0)),
            pl.BlockSpec((CM_GROUPS, CM_CHUNK, CM_CHUNK), lambda i: (0, 0, 0)),
            pl.BlockSpec((CM_GROUPS, CM_CHUNK, 1), lambda i: (0, 0, 0)),
        ],
        out_specs=pl.BlockSpec((tm, width), lambda i: (i, 0)),
        out_shape=jax.ShapeDtypeStruct((t, width), BF16),
        scratch_shapes=[pltpu.VMEM((tm, width), BF16)],
        compiler_params=_params(("parallel",)),
        name="spatial_gate",
    )(uv, uv, ln_g.reshape(1, width), ln_b.reshape(1, width), w_s.astype(BF16),
      b_s.reshape(CM_GROUPS, CM_CHUNK, 1))


def _router_kernel(x_ref, g_ref, mod_ref, wr_ref, br_ref, h_ref, lg_ref):
    m = mod_ref[0]
    h = _norm_mod(x_ref[...], g_ref[...], m[3:4], m[4:5])
    h_ref[...] = h
    h_hi = h.astype(BF16)
    h_lo = (h - h_hi.astype(F32)).astype(BF16)
    w = wr_ref[...]
    w_hi = w.astype(BF16)
    w_lo = (w - w_hi.astype(F32)).astype(BF16)
    lg = (jnp.dot(h_hi, w_hi, preferred_element_type=F32)
          + (jnp.dot(h_lo, w_hi, preferred_element_type=F32) + jnp.dot(h_hi, w_lo, preferred_element_type=F32)))
    lg_ref[...] = lg + br_ref[...]


def _router(x, gamma, mod, w_router_all, b_router_all, *, t_ctx, l_lat):
    t, d = x.shape
    tm = _tile(min(t_ctx, l_lat), 512)
    return pl.pallas_call(
        _router_kernel,
        grid=(t // tm,),
        in_specs=[
            pl.BlockSpec((tm, d), lambda i: (i, 0)),
            pl.BlockSpec((1, d), lambda i: (0, 0)),
            pl.BlockSpec((1, 6, d), lambda i: (_mod_row(i * tm, t_ctx, l_lat), 0, 0)),
            pl.BlockSpec((d, ROUTER_LANES), lambda i: (0, 0)),
            pl.BlockSpec((1, ROUTER_LANES), lambda i: (0, 0)),
        ],
        out_specs=[
            pl.BlockSpec((tm, d), lambda i: (i, 0)),
            pl.BlockSpec((tm, ROUTER_LANES), lambda i: (i, 0)),
        ],
        out_shape=[jax.ShapeDtypeStruct((t, d), F32), jax.ShapeDtypeStruct((t, ROUTER_LANES), F32)],
        compiler_params=_params(("parallel",)),
        name="moe_router",
    )(x, gamma.reshape(1, d), mod, w_router_all, b_router_all)


def _routing_plan(logits, n_blocks):
    t = logits.shape[0]
    group_logits = logits[:, :N_GROUPS]
    expert_logits = logits[:, N_GROUPS:N_GROUPS + N_EXPERTS].reshape(t, N_GROUPS, EXPERTS_PER_GROUP)
    grp = jnp.argmax(group_logits, axis=-1)
    p_grp = jnp.take_along_axis(jax.nn.softmax(group_logits, axis=-1), grp[:, None], axis=-1)
    in_group = jnp.take_along_axis(expert_logits, grp[:, None, None], axis=1)[:, 0]
    top_v, top_i = lax.top_k(in_group, TOP_K)
    weights = jax.nn.softmax(top_v, axis=-1) * p_grp
    expert_ids = (grp[:, None] * EXPERTS_PER_GROUP + top_i).astype(jnp.int32).reshape(-1)
    a = t * TOP_K
    onehot = (expert_ids[:, None] == jnp.arange(N_EXPERTS, dtype=jnp.int32)[None, :]).astype(jnp.int32)
    csum = jnp.cumsum(onehot, axis=0)
    rank = jnp.sum(csum * onehot, axis=1) - 1
    counts = csum[-1]
    padded = (counts + MOE_ROWS - 1) // MOE_ROWS * MOE_ROWS
    pad_ends = jnp.cumsum(padded)
    pad_starts = pad_ends - padded
    pos = (pad_starts[expert_ids] + rank).astype(jnp.int32)
    src_tok = jnp.zeros((n_blocks * MOE_ROWS,), jnp.int32).at[pos].set(jnp.arange(a, dtype=jnp.int32) // TOP_K)
    block_e = jnp.minimum(
        jnp.searchsorted(pad_ends, jnp.arange(n_blocks, dtype=jnp.int32) * MOE_ROWS, side="right"),
        N_EXPERTS - 1).astype(jnp.int32)
    n_used = (pad_ends[-1] // MOE_ROWS).astype(jnp.int32).reshape(1)
    return weights, pos, src_tok, block_e, n_used


def _expert_kernel(be_ref, nu_ref, tok_ref, h_hbm, wg_ref, wu_ref, wd_ref, y_ref, xbuf, sem):
    del be_ref
    i = pl.program_id(0)
    n_used = nu_ref[0]
    rows = xbuf.shape[1]

    def start_gather(blk, slot):
        base = blk * rows

        def body(r, carry):
            tok = tok_ref[base + r]
            pltpu.make_async_copy(h_hbm.at[pl.ds(tok, 1), :], xbuf.at[slot, pl.ds(r, 1), :], sem.at[slot]).start()
            return carry

        lax.fori_loop(0, rows, body, 0)

    def wait_gather(slot):
        pltpu.make_async_copy(h_hbm.at[pl.ds(0, rows), :], xbuf.at[slot], sem.at[slot]).wait()

    slot = lax.rem(i, 2)

    @pl.when(i == 0)
    def _():
        start_gather(0, 0)

    @pl.when(i + 1 < n_used)
    def _():
        start_gather(i + 1, 1 - slot)

    @pl.when(i < n_used)
    def _():
        wait_gather(slot)
        x = xbuf[slot].astype(BF16)
        hg = jnp.dot(x, wg_ref[0], preferred_element_type=F32)
        hu = jnp.dot(x, wu_ref[0], preferred_element_type=F32)
        hm = (jax.nn.silu(hg) * hu).astype(BF16)
        y_ref[...] = jnp.dot(hm, wd_ref[0], preferred_element_type=F32)

    @pl.when(i >= n_used)
    def _():
        y_ref[...] = jnp.zeros_like(y_ref)


def _experts(h2, src_tok, block_e, n_used, w_gate, w_up, w_down, n_blocks):
    t, d = h2.shape
    hid = w_gate.shape[2]
    grid_spec = pltpu.PrefetchScalarGridSpec(
        num_scalar_prefetch=3,
        grid=(n_blocks,),
        in_specs=[
            pl.BlockSpec(memory_space=pl.ANY),
            pl.BlockSpec((1, d, hid), lambda i, be, nu, tok: (be[i], 0, 0)),
            pl.BlockSpec((1, d, hid), lambda i, be, nu, tok: (be[i], 0, 0)),
            pl.BlockSpec((1, hid, d), lambda i, be, nu, tok: (be[i], 0, 0)),
        ],
        out_specs=pl.BlockSpec((MOE_ROWS, d), lambda i, be, nu, tok: (i, 0)),
        scratch_shapes=[pltpu.VMEM((2, MOE_ROWS, d), F32), pltpu.SemaphoreType.DMA((2,))],
    )
    return pl.pallas_call(
        _expert_kernel,
        grid_spec=grid_spec,
        out_shape=jax.ShapeDtypeStruct((n_blocks * MOE_ROWS, d), F32),
        compiler_params=_params(("arbitrary",)),
        name="moe_experts",
    )(block_e, n_used, src_tok, h2, w_gate, w_up, w_down)


def _combine_kernel(pos_ref, x_ref, w_ref, mod_ref, fg_ref, yb_hbm, o_ref, ybuf, sem, *, final_norm):
    i = pl.program_id(0)
    n_tiles = pl.num_programs(0)
    tm = x_ref.shape[0]

    def start_gather(tile, slot):
        base = tile * (tm * TOP_K)

        def body(r, carry):
            for kk in range(TOP_K):
                p = pos_ref[base + TOP_K * r + kk]
                pltpu.make_async_copy(yb_hbm.at[pl.ds(p, 1), :], ybuf.at[slot, pl.ds(kk * tm + r, 1), :],
                                      sem.at[slot]).start()
            return carry

        lax.fori_loop(0, tm, body, 0)

    def wait_gather(slot):
        pltpu.make_async_copy(yb_hbm.at[pl.ds(0, TOP_K * tm), :], ybuf.at[slot], sem.at[slot]).wait()

    slot = lax.rem(i, 2)

    @pl.when(i == 0)
    def _():
        start_gather(0, 0)

    @pl.when(i + 1 < n_tiles)
    def _():
        start_gather(i + 1, 1 - slot)

    wait_gather(slot)
    w = w_ref[...]
    y = w[:, 0:1] * ybuf[slot, 0:tm, :] + w[:, 1:2] * ybuf[slot, tm:2 * tm, :]
    out = x_ref[...] + mod_ref[0][5:6] * y
    if final_norm:
        out = out * lax.rsqrt(jnp.mean(out * out, axis=-1, keepdims=True) + NORM_EPS) * fg_ref[...]
    o_ref[...] = out


def _combine(x, yb, pos, weights, mod, final_g, *, final_norm, t_ctx, l_lat):
    t, d = x.shape
    tm = _tile(min(t_ctx, l_lat), 256)
    grid_spec = pltpu.PrefetchScalarGridSpec(
        num_scalar_prefetch=1,
        grid=(t // tm,),
        in_specs=[
            pl.BlockSpec((tm, d), lambda i, pos: (i, 0)),
            pl.BlockSpec((tm, TOP_K), lambda i, pos: (i, 0)),
            pl.BlockSpec((1, 6, d), lambda i, pos: (_mod_row(i * tm, t_ctx, l_lat), 0, 0)),
            pl.BlockSpec((1, d), lambda i, pos: (0, 0)),
            pl.BlockSpec(memory_space=pl.ANY),
        ],
        out_specs=pl.BlockSpec((tm, d), lambda i, pos: (i, 0)),
        scratch_shapes=[pltpu.VMEM((2, TOP_K * tm, d), F32), pltpu.SemaphoreType.DMA((2,))],
    )
    return pl.pallas_call(
        functools.partial(_combine_kernel, final_norm=final_norm),
        grid_spec=grid_spec,
        out_shape=jax.ShapeDtypeStruct((t, d), F32),
        compiler_params=_params(("arbitrary",)),
        name="moe_combine",
    )(pos, x, weights, mod, final_g.reshape(1, d), yb)


def _moe(x, gamma, mod, w_group, b_group, w_router, b_router, w_gate, w_up, w_down, final_g, *,
         final_norm, t_ctx, l_lat):
    t, d = x.shape
    pad = ROUTER_LANES - N_GROUPS - N_EXPERTS
    w_all = jnp.concatenate([w_group, w_router, jnp.zeros((d, pad), F32)], axis=1)
    b_all = jnp.concatenate([b_group, b_router, jnp.zeros((pad,), F32)]).reshape(1, ROUTER_LANES)
    h2, logits = _router(x, gamma, mod, w_all, b_all, t_ctx=t_ctx, l_lat=l_lat)
    n_blocks = (t * TOP_K) // MOE_ROWS + N_EXPERTS
    weights, pos, src_tok, block_e, n_used = _routing_plan(logits, n_blocks)
    yb = _experts(h2, src_tok, block_e, n_used, w_gate.astype(BF16), w_up.astype(BF16), w_down.astype(BF16),
                  n_blocks)
    return _combine(x, yb, pos, weights, mod, final_g, final_norm=final_norm, t_ctx=t_ctx, l_lat=l_lat)


def kernel(x_prompt, x_sample, c, state_ret, c_ctx, ada_w, ada_b, norm1_g, norm2_g, ret_w_qkvg, ret_w_o,
           ret_gn_g, ret_log_decay, cm_w_in, cm_ln_g, cm_ln_b, cm_w_s, cm_b_s, cm_w_out, moe_w_group,
           moe_b_group, moe_w_router, moe_b_router, moe_w_gate, moe_w_up, moe_w_down, final_norm_g):
    batch, seq, d = x_prompt.shape
    dec_batch, dec_seq, _ = x_sample.shape
    depth = ada_w.shape[0]
    t_ctx = batch * seq
    assert 1 + dec_batch <= COND_ROWS
    tiles = dict(t_ctx=t_ctx, l_lat=dec_seq)

    x = jnp.concatenate([x_prompt.reshape(t_ctx, d), x_sample.reshape(dec_batch * dec_seq, d)], axis=0)
    cond = jnp.concatenate([c_ctx[None, :], c, jnp.zeros((COND_ROWS - 1 - dec_batch, d), F32)], axis=0)
    mod_all = _ada_modulation(cond, ada_w, ada_b).reshape(depth, COND_ROWS, 6, d)
    dk = d // RET_HEADS
    rope = _rope_tables(dec_seq, dk)

    ctx_states = []
    for layer in range(depth):
        mod = mod_all[layer]
        j = layer // 2
        if layer % 2 == 0:
            qkvg = _norm_matmul(x, norm1_g[layer], mod, ret_w_qkvg[j].astype(BF16), shift_idx=0, scale_idx=1,
                                act=None, **tiles)
            log_g = -jnp.exp(ret_log_decay[j].astype(F32))
            o_ctx, s_ctx = _retention(qkvg, log_g, ret_gn_g[j], seq_len=seq, n_seq=batch, seq_off=0,
                                      rope_tables=None, init_state=None, emit_state=True)
            o_lat = _retention(qkvg, log_g, ret_gn_g[j], seq_len=dec_seq, n_seq=dec_batch,
                               seq_off=t_ctx // dec_seq, rope_tables=rope, init_state=(state_ret, j),
                               emit_state=False)
            ctx_states.append(s_ctx)
            a = jnp.concatenate([o_ctx.reshape(t_ctx, -1), o_lat.reshape(dec_batch * dec_seq, -1)], axis=0)
            x = _proj_residual(a, ret_w_o[j].astype(BF16), x, mod, gate_idx=2, **tiles)
        else:
            uv = _norm_matmul(x, norm1_g[layer], mod, cm_w_in[j].astype(BF16), shift_idx=0, scale_idx=1,
                              act="gelu", **tiles)
            gated = _spatial_gate(uv, cm_ln_g[j], cm_ln_b[j], cm_w_s[j], cm_b_s[j])
            x = _proj_residual(gated, cm_w_out[j].astype(BF16), x, mod, gate_idx=2, **tiles)
        x = _moe(x, norm2_g[layer], mod, moe_w_group[layer], moe_b_group[layer], moe_w_router[layer],
                 moe_b_router[layer], moe_w_gate[layer], moe_w_up[layer], moe_w_down[layer], final_norm_g,
                 final_norm=(layer == depth - 1), **tiles)

    y_prompt = x[:t_ctx].reshape(batch, seq, d)
    y_sample = x[t_ctx:].reshape(dec_batch, dec_seq, d)
    new_state_ret = jnp.stack(ctx_states, axis=1)
    return (y_prompt, y_sample, new_state_ret)
```

```python
import functools

import jax
import jax.numpy as jnp
from jax import lax
from jax.experimental import pallas as pl
from jax.experimental.pallas import tpu as pltpu

F32 = jnp.float32
BF16 = jnp.bfloat16
I32 = jnp.int32

RET_HEADS = 8
GRID_W = 64
ROPE_BASE = 10000.0
CM_CHUNK = 128
CM_GROUPS = 8
N_GROUPS = 4
EXPERTS_PER_GROUP = 4
N_EXPERTS = N_GROUPS * EXPERTS_PER_GROUP
TOP_K = 2
NORM_EPS = 1e-6

MOE_ROWS = 256
ROUTER_LANES = 128
COND_ROWS = 16
RET_CHUNK_ROWS = 256
V7X_VMEM_LIMIT = 56 * 1024 * 1024
HI_MASK = -65536


def _params(semantics):
    return pltpu.CompilerParams(dimension_semantics=semantics, vmem_limit_bytes=V7X_VMEM_LIMIT)


def _tile(n, pref):
    t = min(n, pref)
    while n % t:
        t //= 2
    return t


def _pack_pair(a, b):
    a_bits = lax.bitcast_convert_type(a.astype(BF16).astype(F32), I32)
    b_bits = lax.bitcast_convert_type(b.astype(BF16).astype(F32), I32)
    return a_bits | lax.shift_right_logical(b_bits, 16)


def _unpack_pair(w):
    a = lax.bitcast_convert_type(w & HI_MASK, F32)
    b = lax.bitcast_convert_type(lax.shift_left(w, 16), F32)
    return a, b


def _ada_kernel(cond_ref, w_ref, b_ref, o_ref):
    s = jax.nn.silu(cond_ref[...])
    o_ref[0] = jnp.dot(s.astype(BF16), w_ref[0].astype(BF16), preferred_element_type=F32) + b_ref[0]


def _ada_modulation(cond, ada_w, ada_b):
    depth, d, n = ada_w.shape
    tn = _tile(n, 1024)
    return pl.pallas_call(
        _ada_kernel,
        grid=(depth, n // tn),
        in_specs=[
            pl.BlockSpec((COND_ROWS, d), lambda l, j: (0, 0)),
            pl.BlockSpec((1, d, tn), lambda l, j: (l, 0, j)),
            pl.BlockSpec((1, 1, tn), lambda l, j: (l, 0, j)),
        ],
        out_specs=pl.BlockSpec((1, COND_ROWS, tn), lambda l, j: (l, 0, j)),
        out_shape=jax.ShapeDtypeStruct((depth, COND_ROWS, n), F32),
        compiler_params=_params(("parallel", "parallel")),
        name="ada_modulation",
    )(cond, ada_w, ada_b.reshape(depth, 1, n))


def _norm_mod(x, gamma, shift, scale):
    y = x * lax.rsqrt(jnp.mean(x * x, axis=-1, keepdims=True) + NORM_EPS)
    return (y * gamma) * (1.0 + scale) + shift


def _mod_row(tile_start, t_ctx, l_lat):
    return jnp.where(tile_start < t_ctx, 0, 1 + jnp.maximum(tile_start - t_ctx, 0) // l_lat)


def _norm_matmul_kernel(x_ref, g_ref, mod_ref, w_ref, o_ref, h_ref, *, shift_idx, scale_idx, act):
    @pl.when(pl.program_id(1) == 0)
    def _():
        m = mod_ref[0]
        h = _norm_mod(x_ref[...], g_ref[...], m[shift_idx:shift_idx + 1], m[scale_idx:scale_idx + 1])
        h_ref[...] = h.astype(BF16)

    acc = jnp.dot(h_ref[...], w_ref[0], preferred_element_type=F32)
    if act == "gelu":
        acc = jax.nn.gelu(acc)
    o_ref[...] = acc.astype(o_ref.dtype)


def _norm_matmul(x, gamma, mod, w_stack, w_idx, *, shift_idx, scale_idx, act, t_ctx, l_lat):
    t, d = x.shape
    n = w_stack.shape[2]
    tm = _tile(min(t_ctx, l_lat), 1024)
    tn = _tile(n, 1024)
    return pl.pallas_call(
        functools.partial(_norm_matmul_kernel, shift_idx=shift_idx, scale_idx=scale_idx, act=act),
        grid=(t // tm, n // tn),
        in_specs=[
            pl.BlockSpec((tm, d), lambda i, j: (i, 0)),
            pl.BlockSpec((1, d), lambda i, j: (0, 0)),
            pl.BlockSpec((1, 6, d), lambda i, j: (_mod_row(i * tm, t_ctx, l_lat), 0, 0)),
            pl.BlockSpec((1, d, tn), lambda i, j: (w_idx, 0, j)),
        ],
        out_specs=pl.BlockSpec((tm, tn), lambda i, j: (i, j)),
        out_shape=jax.ShapeDtypeStruct((t, n), BF16),
        scratch_shapes=[pltpu.VMEM((tm, d), BF16)],
        compiler_params=_params(("parallel", "arbitrary")),
        name="norm_matmul_" + str(act),
    )(x, gamma.reshape(1, d), mod, w_stack)


def _retention_kernel(*refs, chunk, n_chunks, use_rope, has_init, emit_state, alias_state, zero_rest, dk):
    it = iter(refs)
    lg_ref, q_ref, k_ref, v_ref, g_ref, gn_ref = (next(it) for _ in range(6))
    cos_ref = sin_ref = s0_ref = st_ref = None
    if use_rope:
        cos_ref, sin_ref = next(it), next(it)
    if has_init:
        s0_ref = next(it)
    if alias_state:
        next(it)
    o_ref = next(it)
    if emit_state:
        st_ref = next(it)
    oacc_ref, s_ref = next(it), next(it)

    head = pl.program_id(1)
    lg_f = lg_ref[0, head]
    lg_b = lg_ref[1, head]
    c = chunk
    pos = lax.broadcasted_iota(jnp.int32, (c, 1), 0).astype(F32)
    qd_f = jnp.exp(lg_f * (pos + 1.0))
    kd_f = jnp.exp(lg_f * (c - 1.0 - pos))
    qd_b = jnp.exp(lg_b * (c - pos))
    kd_b = jnp.exp(lg_b * pos)
    cd_f = jnp.exp(jnp.full((1, 1), c, F32) * lg_f)
    cd_b = jnp.exp(jnp.full((1, 1), c, F32) * lg_b)
    rel = (lax.broadcasted_iota(jnp.int32, (c, c), 0) - lax.broadcasted_iota(jnp.int32, (c, c), 1)).astype(F32)
    mask = (jnp.where(rel >= 0, jnp.exp(lg_f * jnp.maximum(rel, 0.0)), 0.0)
            + jnp.where(rel <= 0, jnp.exp(lg_b * jnp.maximum(-rel, 0.0)), 0.0))
    half = dk // 2
    k_scale = dk ** -0.5

    def rows(ci):
        if isinstance(ci, int):
            return pl.ds(ci * c, c)
        return pl.ds(pl.multiple_of(ci * c, c), c)

    def load_qk(sl):
        q = q_ref[0, sl, :].astype(F32)
        k = k_ref[0, sl, :].astype(F32)
        if use_rope:
            cos = cos_ref[sl, :]
            sin = sin_ref[sl, :]

            def rope(x):
                x1, x2 = x[:, :half], x[:, half:]
                return jnp.concatenate([x1 * cos - x2 * sin, x1 * sin + x2 * cos], axis=-1)

            q, k = rope(q), rope(k)
        return q, k * k_scale

    def state_update(k, kd, cd, v):
        kv = lax.dot_general((k * kd).astype(BF16), v, (((0,), (0,)), ((), ())), preferred_element_type=F32)
        return cd * s_ref[...] + kv

    def emit(direction, value):
        st_ref[0, 0, direction, 0] = value
        if zero_rest:
            for other in range(1, st_ref.shape[1]):
                st_ref[0, other, direction, 0] = jnp.zeros_like(value)

    def finish(sl, o):
        mu = jnp.mean(o, axis=-1, keepdims=True)
        dlt = o - mu
        var = jnp.mean(dlt * dlt, axis=-1, keepdims=True)
        on = dlt * lax.rsqrt(var + NORM_EPS)
        gate = g_ref[0, sl, :].astype(F32)
        o_ref[0, sl, :] = ((on * gn_ref[0]) * jax.nn.silu(gate)).astype(o_ref.dtype)

    def intra(q, k, v):
        a = lax.dot_general(q.astype(BF16), k.astype(BF16), (((1,), (1,)), ((), ())),
                            preferred_element_type=F32) * mask
        return jnp.dot(a.astype(BF16), v, preferred_element_type=F32)

    if n_chunks == 1 and not has_init:
        sl = rows(0)
        q, k = load_qk(sl)
        v = v_ref[0, sl, :]
        finish(sl, intra(q, k, v))
        if emit_state:
            s_ref[...] = jnp.zeros_like(s_ref)
            emit(0, state_update(k, kd_f, cd_f, v))
            emit(1, state_update(k, kd_b, cd_b, v))
        return

    if has_init:
        s_ref[...] = s0_ref[0, 0, 1, 0]
    else:
        s_ref[...] = jnp.zeros_like(s_ref)

    def bwd_body(t, carry):
        sl = rows(n_chunks - 1 - t)
        q, k = load_qk(sl)
        v = v_ref[0, sl, :]
        oacc_ref[sl, :] = jnp.dot((q * qd_b).astype(BF16), s_ref[...].astype(BF16), preferred_element_type=F32)
        s_ref[...] = state_update(k, kd_b, cd_b, v)
        return carry

    lax.fori_loop(0, n_chunks, bwd_body, 0)
    if emit_state:
        emit(1, s_ref[...])

    if has_init:
        s_ref[...] = s0_ref[0, 0, 0, 0]
    else:
        s_ref[...] = jnp.zeros_like(s_ref)

    def fwd_body(ci, carry):
        sl = rows(ci)
        q, k = load_qk(sl)
        v = v_ref[0, sl, :]
        o = (intra(q, k, v)
             + jnp.dot((q * qd_f).astype(BF16), s_ref[...].astype(BF16), preferred_element_type=F32)
             + oacc_ref[sl, :])
        finish(sl, o)
        s_ref[...] = state_update(k, kd_f, cd_f, v)
        return carry

    lax.fori_loop(0, n_chunks, fwd_body, 0)
    if emit_state:
        emit(0, s_ref[...])


def _retention(qkvg, log_g, gn_g, *, seq_len, n_seq, seq_off, rope_tables=None, init_state=None,
               state_out=None):
    t, width = qkvg.shape
    h = RET_HEADS
    dk = width // (6 * h)
    dv = 2 * dk
    chunk = min(RET_CHUNK_ROWS, seq_len)
    n_chunks = seq_len // chunk
    x3 = qkvg.reshape(t // seq_len, seq_len, width)
    k_blk, v_blk, g_blk = h, (2 * h * dk) // dv, (2 * h * dk + h * dv) // dv
    use_rope = rope_tables is not None
    has_init = init_state is not None
    emit_state = state_out is not None

    in_specs = [
        pl.BlockSpec(memory_space=pltpu.SMEM),
        pl.BlockSpec((1, seq_len, dk), lambda b, hh: (b + seq_off, 0, hh)),
        pl.BlockSpec((1, seq_len, dk), lambda b, hh: (b + seq_off, 0, k_blk + hh)),
        pl.BlockSpec((1, seq_len, dv), lambda b, hh: (b + seq_off, 0, v_blk + hh)),
        pl.BlockSpec((1, seq_len, dv), lambda b, hh: (b + seq_off, 0, g_blk + hh)),
        pl.BlockSpec((1, 1, dv), lambda b, hh: (hh, 0, 0)),
    ]
    args = [log_g, x3, x3, x3, x3, gn_g.reshape(h, 1, dv)]
    if use_rope:
        in_specs += [pl.BlockSpec((seq_len, dk // 2), lambda b, hh: (0, 0))] * 2
        args += list(rope_tables)
    if has_init:
        layer = init_state[1]
        in_specs.append(pl.BlockSpec((1, 1, 2, 1, dk, dv), lambda b, hh: (b, layer, 0, hh, 0, 0)))
        args.append(init_state[0])
    out_specs = [pl.BlockSpec((1, seq_len, dv), lambda b, hh: (b, 0, hh))]
    out_shape = [jax.ShapeDtypeStruct((n_seq, seq_len, h * dv), BF16)]
    aliases = {}
    alias_state = zero_rest = False
    if emit_state:
        s_layer, n_layers, prev = state_out
        alias_state = prev is not None
        zero_rest = not alias_state and n_layers > 1
        if alias_state:
            aliases = {len(args): 1}
            in_specs.append(pl.BlockSpec(memory_space=pl.ANY))
            args.append(prev)
            out_specs.append(pl.BlockSpec((1, 1, 2, 1, dk, dv), lambda b, hh: (b, s_layer, 0, hh, 0, 0)))
        else:
            assert s_layer == 0
            out_specs.append(pl.BlockSpec((1, n_layers, 2, 1, dk, dv), lambda b, hh: (b, 0, 0, hh, 0, 0)))
        out_shape.append(jax.ShapeDtypeStruct((n_seq, n_layers, 2, h, dk, dv), F32))
    outs = pl.pallas_call(
        functools.partial(_retention_kernel, chunk=chunk, n_chunks=n_chunks, use_rope=use_rope,
                          has_init=has_init, emit_state=emit_state, alias_state=alias_state,
                          zero_rest=zero_rest, dk=dk),
        grid=(n_seq, h),
        in_specs=in_specs,
        out_specs=out_specs,
        out_shape=out_shape,
        scratch_shapes=[pltpu.VMEM((seq_len, dv), F32), pltpu.VMEM((dk, dv), F32)],
        input_output_aliases=aliases,
        compiler_params=_params(("parallel", "parallel")),
        name="retention_rope" if use_rope else "retention_ctx",
    )(*args)
    return outs if emit_state else outs[0]


def _rope_tables(seq_len, dk):
    rows = seq_len // GRID_W
    row = jnp.broadcast_to(jnp.arange(rows)[:, None], (rows, GRID_W)).reshape(-1).astype(F32)
    col = jnp.broadcast_to(jnp.arange(GRID_W)[None, :], (rows, GRID_W)).reshape(-1).astype(F32)
    n_freq = dk // 4
    inv = ROPE_BASE ** (-jnp.arange(n_freq, dtype=F32) / n_freq)
    ang = jnp.concatenate([row[:, None] * inv, col[:, None] * inv], axis=-1)
    return jnp.cos(ang), jnp.sin(ang)


def _proj_residual_kernel(*refs, gate_idx, seg_tiles):
    n_seg = len(seg_tiles)
    a_refs = refs[:n_seg]
    w_ref, x_ref, mod_ref, o_ref = refs[n_seg:]
    i = pl.program_id(1)
    gate = mod_ref[0][gate_idx:gate_idx + 1]
    lo = 0
    for a_ref, n_tiles in zip(a_refs, seg_tiles):
        def seg(a_ref=a_ref):
            acc = jnp.dot(a_ref[...], w_ref[0], preferred_element_type=F32)
            o_ref[...] = x_ref[...] + gate * acc

        if n_seg == 1:
            seg()
        else:
            pl.when(jnp.logical_and(i >= lo, i < lo + n_tiles))(seg)
        lo += n_tiles


def _proj_residual(a_parts, w_stack, w_idx, x, mod, *, gate_idx, t_ctx, l_lat):
    t, d = x.shape
    k = w_stack.shape[1]
    tm = _tile(min(t_ctx, l_lat), 512)
    tn = _tile(d, 1024)
    seg_tiles = tuple(a.shape[0] // tm for a in a_parts)
    assert sum(seg_tiles) * tm == t
    in_specs = []
    lo = 0
    for n_tiles in seg_tiles:
        in_specs.append(pl.BlockSpec(
            (tm, k), lambda j, i, lo=lo, n_tiles=n_tiles: (jnp.clip(i - lo, 0, n_tiles - 1), 0)))
        lo += n_tiles
    in_specs += [
        pl.BlockSpec((1, k, tn), lambda j, i: (w_idx, 0, j)),
        pl.BlockSpec((tm, tn), lambda j, i: (i, j)),
        pl.BlockSpec((1, 6, tn), lambda j, i: (_mod_row(i * tm, t_ctx, l_lat), 0, j)),
    ]
    return pl.pallas_call(
        functools.partial(_proj_residual_kernel, gate_idx=gate_idx, seg_tiles=seg_tiles),
        grid=(d // tn, t // tm),
        in_specs=in_specs,
        out_specs=pl.BlockSpec((tm, tn), lambda j, i: (i, j)),
        out_shape=jax.ShapeDtypeStruct((t, d), F32),
        compiler_params=_params(("parallel", "parallel")),
        name="proj_residual",
    )(*a_parts, w_stack, x, mod)


def _spatial_gate_kernel(u_ref, v_ref, lng_ref, lnb_ref, ws_ref, bs_ref, t_ref, vn_ref, *, n_sub, cw):
    v32 = v_ref[...].astype(F32)
    mu = jnp.mean(v32, axis=-1, keepdims=True)
    dlt = v32 - mu
    var = jnp.mean(dlt * dlt, axis=-1, keepdims=True)
    vn_ref[...] = (dlt * lax.rsqrt(var + NORM_EPS) * lng_ref[...] + lnb_ref[...]).astype(BF16)
    for n in range(n_sub):
        rs = slice(n * CM_CHUNK, (n + 1) * CM_CHUNK)
        for g in range(CM_GROUPS):
            cs = slice(g * cw, (g + 1) * cw)
            s = jnp.dot(ws_ref[g], vn_ref[rs, cs], preferred_element_type=F32) + bs_ref[g]
            t_ref[rs, cs] = (u_ref[rs, cs].astype(F32) * s).astype(t_ref.dtype)


def _spatial_gate(uv, ln_g, ln_b, w_s, b_s):
    t, two_cw = uv.shape
    width = two_cw // 2
    cw = width // CM_GROUPS
    tm = 2 * CM_CHUNK
    return pl.pallas_call(
        functools.partial(_spatial_gate_kernel, n_sub=tm // CM_CHUNK, cw=cw),
        grid=(t // tm,),
        in_specs=[
            pl.BlockSpec((tm, width), lambda i: (i, 0)),
            pl.BlockSpec((tm, width), lambda i: (i, 1)),
            pl.BlockSpec((1, width), lambda i: (0, 0)),
            pl.BlockSpec((1, width), lambda i: (0, 0)),
            pl.BlockSpec((CM_GROUPS, CM_CHUNK, CM_CHUNK), lambda i: (0, 0, 0)),
            pl.BlockSpec((CM_GROUPS, CM_CHUNK, 1), lambda i: (0, 0, 0)),
        ],
        out_specs=pl.BlockSpec((tm, width), lambda i: (i, 0)),
        out_shape=jax.ShapeDtypeStruct((t, width), BF16),
        scratch_shapes=[pltpu.VMEM((tm, width), BF16)],
        compiler_params=_params(("parallel",)),
        name="spatial_gate",
    )(uv, uv, ln_g.reshape(1, width), ln_b.reshape(1, width), w_s.astype(BF16),
      b_s.reshape(CM_GROUPS, CM_CHUNK, 1))


def _router_kernel(x_ref, g_ref, mod_ref, wr_ref, br_ref, h_ref, lg_ref):
    m = mod_ref[0]
    h = _norm_mod(x_ref[...], g_ref[...], m[3:4], m[4:5])
    hp = h.shape[1] // 2
    h_ref[...] = _pack_pair(h[:, :hp], h[:, hp:])
    h_hi = h.astype(BF16)
    h_lo = (h - h_hi.astype(F32)).astype(BF16)
    w = wr_ref[...]
    w_hi = w.astype(BF16)
    w_lo = (w - w_hi.astype(F32)).astype(BF16)
    lg = (jnp.dot(h_hi, w_hi, preferred_element_type=F32)
          + (jnp.dot(h_lo, w_hi, preferred_element_type=F32) + jnp.dot(h_hi, w_lo, preferred_element_type=F32)))
    lg_ref[...] = lg + br_ref[...]


def _router(x, gamma, mod, w_router_all, b_router_all, *, t_ctx, l_lat):
    t, d = x.shape
    tm = _tile(min(t_ctx, l_lat), 512)
    return pl.pallas_call(
        _router_kernel,
        grid=(t // tm,),
        in_specs=[
            pl.BlockSpec((tm, d), lambda i: (i, 0)),
            pl.BlockSpec((1, d), lambda i: (0, 0)),
            pl.BlockSpec((1, 6, d), lambda i: (_mod_row(i * tm, t_ctx, l_lat), 0, 0)),
            pl.BlockSpec((d, ROUTER_LANES), lambda i: (0, 0)),
            pl.BlockSpec((1, ROUTER_LANES), lambda i: (0, 0)),
        ],
        out_specs=[
            pl.BlockSpec((tm, d // 2), lambda i: (i, 0)),
            pl.BlockSpec((tm, ROUTER_LANES), lambda i: (i, 0)),
        ],
        out_shape=[jax.ShapeDtypeStruct((t, d // 2), I32), jax.ShapeDtypeStruct((t, ROUTER_LANES), F32)],
        compiler_params=_params(("parallel",)),
        name="moe_router",
    )(x, gamma.reshape(1, d), mod, w_router_all, b_router_all)


def _routing_plan(logits, n_blocks):
    t = logits.shape[0]
    group_logits = logits[:, :N_GROUPS]
    expert_logits = logits[:, N_GROUPS:N_GROUPS + N_EXPERTS].reshape(t, N_GROUPS, EXPERTS_PER_GROUP)
    grp = jnp.argmax(group_logits, axis=-1)
    p_grp = jnp.take_along_axis(jax.nn.softmax(group_logits, axis=-1), grp[:, None], axis=-1)
    in_group = jnp.take_along_axis(expert_logits, grp[:, None, None], axis=1)[:, 0]
    top_v, top_i = lax.top_k(in_group, TOP_K)
    weights = jax.nn.softmax(top_v, axis=-1) * p_grp
    expert_ids = (grp[:, None] * EXPERTS_PER_GROUP + top_i).astype(I32).reshape(-1)
    e_range = jnp.arange(N_EXPERTS, dtype=I32)
    onehot = (expert_ids[:, None] == e_range[None, :]).astype(I32)
    csum = jnp.cumsum(onehot, axis=0)
    rank = jnp.sum(csum * onehot, axis=1) - 1
    counts = csum[-1]
    padded = (counts + MOE_ROWS - 1) // MOE_ROWS * MOE_ROWS
    pad_ends = jnp.cumsum(padded)
    pad_starts = pad_ends - padded
    pos = (jnp.sum(onehot * pad_starts[None, :], axis=1) + rank).astype(I32)
    blk_start = jnp.arange(n_blocks, dtype=I32) * MOE_ROWS
    block_e = jnp.minimum(jnp.sum((pad_ends[None, :] <= blk_start[:, None]).astype(I32), axis=1),
                          N_EXPERTS - 1).astype(I32)
    n_used = (pad_ends[-1] // MOE_ROWS).astype(I32).reshape(1)
    later = jnp.where((counts[None, :] > 0) & (e_range[None, :] > e_range[:, None]), e_range[None, :], N_EXPERTS)
    next_of = jnp.min(later, axis=1)
    next_of = jnp.where(next_of == N_EXPERTS, -1, next_of).astype(I32)
    next_e = jnp.sum((block_e[:, None] == e_range[None, :]).astype(I32) * next_of[None, :], axis=1).astype(I32)
    pad_lo = (pad_starts + counts).astype(I32)
    return weights, pos, block_e, next_e, n_used, pad_lo, pad_ends.astype(I32)


def _dispatch_kernel(pos_ref, lo_ref, hi_ref, h_ref, xb_hbm, zero_ref, sem, zsem):
    i = pl.program_id(0)
    tm = h_ref.shape[0]
    base = i * (tm * TOP_K)

    def body(r, carry):
        for kk in range(TOP_K):
            p = pos_ref[base + TOP_K * r + kk]
            pltpu.make_async_copy(h_ref.at[pl.ds(r, 1), :], xb_hbm.at[pl.ds(p, 1), :], sem).start()
        return carry

    lax.fori_loop(0, tm, body, 0, unroll=8)
    for _ in range(TOP_K):
        pltpu.make_async_copy(h_ref, xb_hbm.at[pl.ds(0, tm), :], sem).wait()

    @pl.when(i == pl.num_programs(0) - 1)
    def _():
        zero_ref[...] = jnp.zeros_like(zero_ref)

        def zero_row(p):
            return pltpu.make_async_copy(zero_ref.at[pl.ds(0, 1), :], xb_hbm.at[pl.ds(p, 1), :], zsem)

        def per_expert(e, carry):
            lo = lo_ref[e]
            hi = hi_ref[e]

            def start(p, c):
                zero_row(p).start()
                return c

            def wait(p, c):
                zero_row(p).wait()
                return c

            lax.fori_loop(lo, hi, start, 0)
            lax.fori_loop(lo, hi, wait, 0)
            return carry

        lax.fori_loop(0, N_EXPERTS, per_expert, 0)

        n_blocks = xb_hbm.shape[0] // MOE_ROWS
        n_used = hi_ref[N_EXPERTS - 1] // MOE_ROWS

        def zero_block(blk):
            dst = xb_hbm.at[pl.ds(pl.multiple_of(blk * MOE_ROWS, MOE_ROWS), MOE_ROWS), :]
            return pltpu.make_async_copy(zero_ref, dst, zsem)

        def start_block(blk, c):
            zero_block(blk).start()
            return c

        def wait_block(blk, c):
            zero_block(blk).wait()
            return c

        lax.fori_loop(n_used, n_blocks, start_block, 0)
        lax.fori_loop(n_used, n_blocks, wait_block, 0)


def _dispatch(hp, pos, pad_lo, pad_hi, n_blocks):
    t, width = hp.shape
    tm = _tile(t, 512)
    grid_spec = pltpu.PrefetchScalarGridSpec(
        num_scalar_prefetch=3,
        grid=(t // tm,),
        in_specs=[pl.BlockSpec((tm, width), lambda i, pos, lo, hi: (i, 0))],
        out_specs=pl.BlockSpec(memory_space=pl.ANY),
        scratch_shapes=[pltpu.VMEM((MOE_ROWS, width), I32), pltpu.SemaphoreType.DMA(()),
                        pltpu.SemaphoreType.DMA(())],
    )
    return pl.pallas_call(
        _dispatch_kernel,
        grid_spec=grid_spec,
        out_shape=jax.ShapeDtypeStruct((n_blocks * MOE_ROWS, width), I32),
        compiler_params=_params(("arbitrary",)),
        name="moe_dispatch",
    )(pos, pad_lo, pad_hi, hp)


def _expert_kernel(be_ref, nx_ref, nu_ref, x_ref, wg_hbm, wu_hbm, wd_hbm, y_ref,
                   sg_ref, su_ref, sd_ref, wg_ref, wu_ref, wd_ref, sem, *, layer):
    i = pl.program_id(0)
    n_used = nu_ref[0]
    e = be_ref[i]
    changed = jnp.logical_or(i == 0, e != be_ref[jnp.maximum(i - 1, 0)])
    hp = x_ref.shape[1]

    def weight_copies(ex):
        return (pltpu.make_async_copy(wg_hbm.at[layer, ex], sg_ref, sem.at[0]),
                pltpu.make_async_copy(wu_hbm.at[layer, ex], su_ref, sem.at[1]),
                pltpu.make_async_copy(wd_hbm.at[layer, ex], sd_ref, sem.at[2]))

    @pl.when(i == 0)
    def _():
        for cp in weight_copies(e):
            cp.start()

    @pl.when(jnp.logical_and(changed, i < n_used))
    def _():
        for cp in weight_copies(e):
            cp.wait()
        for src, dst in ((sg_ref, wg_ref), (su_ref, wu_ref), (sd_ref, wd_ref)):
            rows = src.shape[0]
            step = 256

            def cast(ci, carry, src=src, dst=dst):
                sl = pl.ds(pl.multiple_of(ci * step, step), step)
                dst[sl, :] = src[sl, :].astype(BF16)
                return carry

            lax.fori_loop(0, rows // step, cast, 0)
        nxt = nx_ref[i]

        @pl.when(nxt >= 0)
        def _():
            for cp in weight_copies(nxt):
                cp.start()

    @pl.when(i < n_used)
    def _():
        xa, xb = _unpack_pair(x_ref[...])
        x = jnp.concatenate([xa.astype(BF16), xb.astype(BF16)], axis=1)
        hg = jnp.dot(x, wg_ref[...], preferred_element_type=F32)
        hu = jnp.dot(x, wu_ref[...], preferred_element_type=F32)
        hm = (jax.nn.silu(hg) * hu).astype(BF16)
        y = jnp.dot(hm, wd_ref[...], preferred_element_type=F32)
        y_ref[...] = _pack_pair(y[:, :hp], y[:, hp:])

    @pl.when(i >= n_used)
    def _():
        y_ref[...] = jnp.zeros_like(y_ref)


def _experts(xb, block_e, next_e, n_used, w_gate, w_up, w_down, layer, n_blocks):
    width = xb.shape[1]
    _, _, d, hid = w_gate.shape
    grid_spec = pltpu.PrefetchScalarGridSpec(
        num_scalar_prefetch=3,
        grid=(n_blocks,),
        in_specs=[
            pl.BlockSpec((MOE_ROWS, width), lambda i, be, nx, nu: (jnp.minimum(i, nu[0] - 1), 0)),
            pl.BlockSpec(memory_space=pl.ANY),
            pl.BlockSpec(memory_space=pl.ANY),
            pl.BlockSpec(memory_space=pl.ANY),
        ],
        out_specs=pl.BlockSpec((MOE_ROWS, width), lambda i, be, nx, nu: (i, 0)),
        scratch_shapes=[
            pltpu.VMEM((d, hid), F32), pltpu.VMEM((d, hid), F32), pltpu.VMEM((hid, d), F32),
            pltpu.VMEM((d, hid), BF16), pltpu.VMEM((d, hid), BF16), pltpu.VMEM((hid, d), BF16),
            pltpu.SemaphoreType.DMA((3,)),
        ],
    )
    return pl.pallas_call(
        functools.partial(_expert_kernel, layer=layer),
        grid_spec=grid_spec,
        out_shape=jax.ShapeDtypeStruct((n_blocks * MOE_ROWS, width), I32),
        compiler_params=_params(("arbitrary",)),
        name="moe_experts",
    )(block_e, next_e, n_used, xb, w_gate, w_up, w_down)


def _combine_kernel(pos_ref, x_ref, w_ref, mod_ref, fg_ref, yb_hbm, o_ref, ybuf, sem, *, final_norm):
    i = pl.program_id(0)
    n_tiles = pl.num_programs(0)
    tm = x_ref.shape[0]
    hp = ybuf.shape[2]

    def start_gather(tile, slot):
        base = tile * (tm * TOP_K)

        def body(r, carry):
            for kk in range(TOP_K):
                p = pos_ref[base + TOP_K * r + kk]
                pltpu.make_async_copy(yb_hbm.at[pl.ds(p, 1), :], ybuf.at[slot, pl.ds(kk * tm + r, 1), :],
                                      sem.at[slot]).start()
            return carry

        lax.fori_loop(0, tm, body, 0, unroll=8)

    def wait_gather(slot):
        pltpu.make_async_copy(yb_hbm.at[pl.ds(0, TOP_K * tm), :], ybuf.at[slot], sem.at[slot]).wait()

    slot = lax.rem(i, 2)

    @pl.when(i == 0)
    def _():
        start_gather(0, 0)

    @pl.when(i + 1 < n_tiles)
    def _():
        start_gather(i + 1, 1 - slot)

    wait_gather(slot)
    w = w_ref[...]
    a0, b0 = _unpack_pair(ybuf[slot, 0:tm, :])
    a1, b1 = _unpack_pair(ybuf[slot, tm:2 * tm, :])
    gate = mod_ref[0][5:6]
    out_a = x_ref[:, :hp] + gate[:, :hp] * (w[:, 0:1] * a0 + w[:, 1:2] * a1)
    out_b = x_ref[:, hp:] + gate[:, hp:] * (w[:, 0:1] * b0 + w[:, 1:2] * b1)
    if final_norm:
        ms = (jnp.sum(out_a * out_a, axis=-1, keepdims=True)
              + jnp.sum(out_b * out_b, axis=-1, keepdims=True)) / (2 * hp)
        inv = lax.rsqrt(ms + NORM_EPS)
        out_a = out_a * inv * fg_ref[:, :hp]
        out_b = out_b * inv * fg_ref[:, hp:]
    o_ref[:, :hp] = out_a
    o_ref[:, hp:] = out_b


def _combine(x, yb, pos, weights, mod, final_g, *, final_norm, t_ctx, l_lat):
    t, d = x.shape
    tm = _tile(min(t_ctx, l_lat), 256)
    grid_spec = pltpu.PrefetchScalarGridSpec(
        num_scalar_prefetch=1,
        grid=(t // tm,),
        in_specs=[
            pl.BlockSpec((tm, d), lambda i, pos: (i, 0)),
            pl.BlockSpec((tm, TOP_K), lambda i, pos: (i, 0)),
            pl.BlockSpec((1, 6, d), lambda i, pos: (_mod_row(i * tm, t_ctx, l_lat), 0, 0)),
            pl.BlockSpec((1, d), lambda i, pos: (0, 0)),
            pl.BlockSpec(memory_space=pl.ANY),
        ],
        out_specs=pl.BlockSpec((tm, d), lambda i, pos: (i, 0)),
        scratch_shapes=[pltpu.VMEM((2, TOP_K * tm, d // 2), I32), pltpu.SemaphoreType.DMA((2,))],
    )
    return pl.pallas_call(
        functools.partial(_combine_kernel, final_norm=final_norm),
        grid_spec=grid_spec,
        out_shape=jax.ShapeDtypeStruct((t, d), F32),
        compiler_params=_params(("arbitrary",)),
        name="moe_combine",
    )(pos, x, weights, mod, final_g.reshape(1, d), yb)


def _moe(x, gamma, mod, layer, w_group, b_group, w_router, b_router, w_gate, w_up, w_down, final_g, *,
         final_norm, t_ctx, l_lat):
    t, d = x.shape
    pad = ROUTER_LANES - N_GROUPS - N_EXPERTS
    w_all = jnp.concatenate([w_group[layer], w_router[layer], jnp.zeros((d, pad), F32)], axis=1)
    b_all = jnp.concatenate([b_group[layer], b_router[layer], jnp.zeros((pad,), F32)]).reshape(1, ROUTER_LANES)
    hp, logits = _router(x, gamma, mod, w_all, b_all, t_ctx=t_ctx, l_lat=l_lat)
    n_blocks = (t * TOP_K) // MOE_ROWS + N_EXPERTS
    weights, pos, block_e, next_e, n_used, pad_lo, pad_hi = _routing_plan(logits, n_blocks)
    xb = _dispatch(hp, pos, pad_lo, pad_hi, n_blocks)
    yb = _experts(xb, block_e, next_e, n_used, w_gate, w_up, w_down, layer, n_blocks)
    return _combine(x, yb, pos, weights, mod, final_g, final_norm=final_norm, t_ctx=t_ctx, l_lat=l_lat)


def kernel(x_prompt, x_sample, c, state_ret, c_ctx, ada_w, ada_b, norm1_g, norm2_g, ret_w_qkvg, ret_w_o,
           ret_gn_g, ret_log_decay, cm_w_in, cm_ln_g, cm_ln_b, cm_w_s, cm_b_s, cm_w_out, moe_w_group,
           moe_b_group, moe_w_router, moe_b_router, moe_w_gate, moe_w_up, moe_w_down, final_norm_g):
    batch, seq, d = x_prompt.shape
    dec_batch, dec_seq, _ = x_sample.shape
    depth = ada_w.shape[0]
    n_ret = ret_w_qkvg.shape[0]
    t_ctx = batch * seq
    assert 1 + dec_batch <= COND_ROWS
    tiles = dict(t_ctx=t_ctx, l_lat=dec_seq)

    x = jnp.concatenate([x_prompt.reshape(t_ctx, d), x_sample.reshape(dec_batch * dec_seq, d)], axis=0)
    cond = jnp.concatenate([c_ctx[None, :], c, jnp.zeros((COND_ROWS - 1 - dec_batch, d), F32)], axis=0)
    mod_all = _ada_modulation(cond, ada_w, ada_b).reshape(depth, COND_ROWS, 6, d)
    dk = d // RET_HEADS
    rope = _rope_tables(dec_seq, dk)
    w_qkvg, w_o = ret_w_qkvg.astype(BF16), ret_w_o.astype(BF16)
    w_in, w_out = cm_w_in.astype(BF16), cm_w_out.astype(BF16)

    states = None
    for layer in range(depth):
        mod = mod_all[layer]
        j = layer // 2
        if layer % 2 == 0:
            qkvg = _norm_matmul(x, norm1_g[layer], mod, w_qkvg, j, shift_idx=0, scale_idx=1, act=None, **tiles)
            log_g = -jnp.exp(ret_log_decay[j].astype(F32))
            o_ctx, states = _retention(qkvg, log_g, ret_gn_g[j], seq_len=seq, n_seq=batch, seq_off=0,
                                       state_out=(j, n_ret, states))
            o_lat = _retention(qkvg, log_g, ret_gn_g[j], seq_len=dec_seq, n_seq=dec_batch,
                               seq_off=t_ctx // dec_seq, rope_tables=rope, init_state=(state_ret, j))
            a_parts = [o_ctx.reshape(t_ctx, -1), o_lat.reshape(dec_batch * dec_seq, -1)]
            x = _proj_residual(a_parts, w_o, j, x, mod, gate_idx=2, **tiles)
        else:
            uv = _norm_matmul(x, norm1_g[layer], mod, w_in, j, shift_idx=0, scale_idx=1, act="gelu", **tiles)
            gated = _spatial_gate(uv, cm_ln_g[j], cm_ln_b[j], cm_w_s[j], cm_b_s[j])
            x = _proj_residual([gated], w_out, j, x, mod, gate_idx=2, **tiles)
        x = _moe(x, norm2_g[layer], mod, layer, moe_w_group, moe_b_group, moe_w_router, moe_b_router,
                 moe_w_gate, moe_w_up, moe_w_down, final_norm_g, final_norm=(layer == depth - 1), **tiles)

    y_prompt = x[:t_ctx].reshape(batch, seq, d)
    y_sample = x[t_ctx:].reshape(dec_batch, dec_seq, d)
    return (y_prompt, y_sample, states)
```

```python
import functools

import jax
import jax.numpy as jnp
from jax import lax
from jax.experimental import pallas as pl
from jax.experimental.pallas import tpu as pltpu

F32 = jnp.float32
BF16 = jnp.bfloat16
I32 = jnp.int32

RET_HEADS = 8
GRID_W = 64
ROPE_BASE = 10000.0
CM_CHUNK = 128
CM_GROUPS = 8
N_GROUPS = 4
EXPERTS_PER_GROUP = 4
N_EXPERTS = N_GROUPS * EXPERTS_PER_GROUP
TOP_K = 2
NORM_EPS = 1e-6

MOE_ROWS = 256
ROUTER_LANES = 128
COND_ROWS = 16
RET_CHUNK_ROWS = 256
V7X_VMEM_LIMIT = 56 * 1024 * 1024
HI_MASK = -65536
NEG_LOGIT = -3.0e38


def _params(semantics):
    return pltpu.CompilerParams(dimension_semantics=semantics, vmem_limit_bytes=V7X_VMEM_LIMIT)


def _tile(n, pref):
    t = min(n, pref)
    while n % t:
        t //= 2
    return t


def _pack_pair(a, b):
    a_bits = lax.bitcast_convert_type(a.astype(BF16).astype(F32), I32)
    b_bits = lax.bitcast_convert_type(b.astype(BF16).astype(F32), I32)
    return a_bits | lax.shift_right_logical(b_bits, 16)


def _unpack_pair(w):
    a = lax.bitcast_convert_type(w & HI_MASK, F32)
    b = lax.bitcast_convert_type(lax.shift_left(w, 16), F32)
    return a, b


def _ada_kernel(cond_ref, w_ref, b_ref, o_ref):
    s = jax.nn.silu(cond_ref[...])
    o_ref[0] = jnp.dot(s.astype(BF16), w_ref[0].astype(BF16), preferred_element_type=F32) + b_ref[0]


def _ada_modulation(cond, ada_w, ada_b):
    depth, d, n = ada_w.shape
    tn = _tile(n, 1024)
    return pl.pallas_call(
        _ada_kernel,
        grid=(depth, n // tn),
        in_specs=[
            pl.BlockSpec((COND_ROWS, d), lambda l, j: (0, 0)),
            pl.BlockSpec((1, d, tn), lambda l, j: (l, 0, j)),
            pl.BlockSpec((1, 1, tn), lambda l, j: (l, 0, j)),
        ],
        out_specs=pl.BlockSpec((1, COND_ROWS, tn), lambda l, j: (l, 0, j)),
        out_shape=jax.ShapeDtypeStruct((depth, COND_ROWS, n), F32),
        compiler_params=_params(("parallel", "parallel")),
        name="ada_modulation",
    )(cond, ada_w, ada_b.reshape(depth, 1, n))


def _norm_mod(x, gamma, shift, scale):
    y = x * lax.rsqrt(jnp.mean(x * x, axis=-1, keepdims=True) + NORM_EPS)
    return (y * gamma) * (1.0 + scale) + shift


def _mod_row(tile_start, t_ctx, l_lat):
    return jnp.where(tile_start < t_ctx, 0, 1 + jnp.maximum(tile_start - t_ctx, 0) // l_lat)


def _part_specs(parts, tm, index_of, single_buffer=False):
    specs, seg_tiles, lo = [], [], 0
    extra = dict(pipeline_mode=pl.Buffered(1)) if single_buffer and len(parts) > 1 else {}
    for p in parts:
        n_tiles = p.shape[0] // tm
        specs.append(pl.BlockSpec(
            (tm, p.shape[1]),
            lambda *ids, lo=lo, n_tiles=n_tiles: (jnp.clip(index_of(*ids) - lo, 0, n_tiles - 1), 0), **extra))
        seg_tiles.append(n_tiles)
        lo += n_tiles
    return specs, tuple(seg_tiles)


def _for_each_part(i, seg_tiles, fn):
    if len(seg_tiles) == 1:
        fn(0)
        return
    lo = 0
    for s, n_tiles in enumerate(seg_tiles):
        pl.when(jnp.logical_and(i >= lo, i < lo + n_tiles))(functools.partial(fn, s))
        lo += n_tiles


def _norm_matmul_kernel(*refs, shift_idx, scale_idx, act, seg_tiles):
    n_seg = len(seg_tiles)
    x_refs = refs[:n_seg]
    g_ref, mod_ref, w_ref, o_ref, h_ref = refs[n_seg:]

    @pl.when(pl.program_id(1) == 0)
    def _():
        def prologue(s):
            m = mod_ref[0]
            h = _norm_mod(x_refs[s][...], g_ref[...], m[shift_idx:shift_idx + 1], m[scale_idx:scale_idx + 1])
            h_ref[...] = h.astype(BF16)

        _for_each_part(pl.program_id(0), seg_tiles, prologue)

    acc = jnp.dot(h_ref[...], w_ref[0], preferred_element_type=F32)
    if act == "gelu":
        acc = jax.nn.gelu(acc)
    o_ref[...] = acc.astype(o_ref.dtype)


def _norm_matmul(x_parts, gamma, mod, w_stack, w_idx, *, shift_idx, scale_idx, act, t_ctx, l_lat):
    d = x_parts[0].shape[1]
    t = sum(p.shape[0] for p in x_parts)
    n = w_stack.shape[2]
    tm = _tile(min(t_ctx, l_lat), 1024)
    tn = _tile(n, 1024)
    x_specs, seg_tiles = _part_specs(x_parts, tm, lambda i, j: i, single_buffer=True)
    return pl.pallas_call(
        functools.partial(_norm_matmul_kernel, shift_idx=shift_idx, scale_idx=scale_idx, act=act,
                          seg_tiles=seg_tiles),
        grid=(t // tm, n // tn),
        in_specs=x_specs + [
            pl.BlockSpec((1, d), lambda i, j: (0, 0)),
            pl.BlockSpec((1, 6, d), lambda i, j: (_mod_row(i * tm, t_ctx, l_lat), 0, 0)),
            pl.BlockSpec((1, d, tn), lambda i, j: (w_idx, 0, j)),
        ],
        out_specs=pl.BlockSpec((tm, tn), lambda i, j: (i, j)),
        out_shape=jax.ShapeDtypeStruct((t, n), BF16),
        scratch_shapes=[pltpu.VMEM((tm, d), BF16)],
        compiler_params=_params(("parallel", "arbitrary")),
        name="norm_matmul_" + str(act),
    )(*x_parts, gamma.reshape(1, d), mod, w_stack)


def _retention_kernel(*refs, chunk, n_chunks, use_rope, has_init, emit_state, alias_state, zero_rest, dk):
    it = iter(refs)
    lg_ref, q_ref, k_ref, v_ref, g_ref, gn_ref = (next(it) for _ in range(6))
    cos_ref = sin_ref = s0_ref = st_ref = None
    if use_rope:
        cos_ref, sin_ref = next(it), next(it)
    if has_init:
        s0_ref = next(it)
    if alias_state:
        next(it)
    o_ref = next(it)
    if emit_state:
        st_ref = next(it)
    oacc_ref, s_ref = next(it), next(it)

    head = pl.program_id(1)
    lg_f = lg_ref[0, head]
    lg_b = lg_ref[1, head]
    c = chunk
    pos = lax.broadcasted_iota(jnp.int32, (c, 1), 0).astype(F32)
    qd_f = jnp.exp(lg_f * (pos + 1.0))
    kd_f = jnp.exp(lg_f * (c - 1.0 - pos))
    qd_b = jnp.exp(lg_b * (c - pos))
    kd_b = jnp.exp(lg_b * pos)
    cd_f = jnp.exp(jnp.full((1, 1), c, F32) * lg_f)
    cd_b = jnp.exp(jnp.full((1, 1), c, F32) * lg_b)
    rel = (lax.broadcasted_iota(jnp.int32, (c, c), 0) - lax.broadcasted_iota(jnp.int32, (c, c), 1)).astype(F32)
    mask = (jnp.where(rel >= 0, jnp.exp(lg_f * jnp.maximum(rel, 0.0)), 0.0)
            + jnp.where(rel <= 0, jnp.exp(lg_b * jnp.maximum(-rel, 0.0)), 0.0))
    half = dk // 2
    k_scale = dk ** -0.5

    def rows(ci):
        if isinstance(ci, int):
            return pl.ds(ci * c, c)
        return pl.ds(pl.multiple_of(ci * c, c), c)

    def load_qk(sl):
        q = q_ref[0, sl, :].astype(F32)
        k = k_ref[0, sl, :].astype(F32)
        if use_rope:
            cos = cos_ref[sl, :]
            sin = sin_ref[sl, :]

            def rope(x):
                x1, x2 = x[:, :half], x[:, half:]
                return jnp.concatenate([x1 * cos - x2 * sin, x1 * sin + x2 * cos], axis=-1)

            q, k = rope(q), rope(k)
        return q, k * k_scale

    def state_update(k, kd, cd, v):
        kv = lax.dot_general((k * kd).astype(BF16), v, (((0,), (0,)), ((), ())), preferred_element_type=F32)
        return cd * s_ref[...] + kv

    def emit(direction, value):
        st_ref[0, 0, direction, 0] = value
        if zero_rest:
            for other in range(1, st_ref.shape[1]):
                st_ref[0, other, direction, 0] = jnp.zeros_like(value)

    def finish(sl, o):
        mu = jnp.mean(o, axis=-1, keepdims=True)
        dlt = o - mu
        var = jnp.mean(dlt * dlt, axis=-1, keepdims=True)
        on = dlt * lax.rsqrt(var + NORM_EPS)
        gate = g_ref[0, sl, :].astype(F32)
        o_ref[0, sl, :] = ((on * gn_ref[0]) * jax.nn.silu(gate)).astype(o_ref.dtype)

    def intra(q, k, v):
        a = lax.dot_general(q.astype(BF16), k.astype(BF16), (((1,), (1,)), ((), ())),
                            preferred_element_type=F32) * mask
        return jnp.dot(a.astype(BF16), v, preferred_element_type=F32)

    if n_chunks == 1 and not has_init:
        sl = rows(0)
        q, k = load_qk(sl)
        v = v_ref[0, sl, :]
        finish(sl, intra(q, k, v))
        if emit_state:
            s_ref[...] = jnp.zeros_like(s_ref)
            emit(0, state_update(k, kd_f, cd_f, v))
            emit(1, state_update(k, kd_b, cd_b, v))
        return

    if has_init:
        s_ref[...] = s0_ref[0, 0, 1, 0]
    else:
        s_ref[...] = jnp.zeros_like(s_ref)

    def bwd_body(t, carry):
        sl = rows(n_chunks - 1 - t)
        q, k = load_qk(sl)
        v = v_ref[0, sl, :]
        oacc_ref[sl, :] = jnp.dot((q * qd_b).astype(BF16), s_ref[...].astype(BF16), preferred_element_type=F32)
        s_ref[...] = state_update(k, kd_b, cd_b, v)
        return carry

    lax.fori_loop(0, n_chunks, bwd_body, 0)
    if emit_state:
        emit(1, s_ref[...])

    if has_init:
        s_ref[...] = s0_ref[0, 0, 0, 0]
    else:
        s_ref[...] = jnp.zeros_like(s_ref)

    def fwd_body(ci, carry):
        sl = rows(ci)
        q, k = load_qk(sl)
        v = v_ref[0, sl, :]
        o = (intra(q, k, v)
             + jnp.dot((q * qd_f).astype(BF16), s_ref[...].astype(BF16), preferred_element_type=F32)
             + oacc_ref[sl, :])
        finish(sl, o)
        s_ref[...] = state_update(k, kd_f, cd_f, v)
        return carry

    lax.fori_loop(0, n_chunks, fwd_body, 0)
    if emit_state:
        emit(0, s_ref[...])


def _retention(qkvg, log_g, gn_g, *, seq_len, n_seq, seq_off, rope_tables=None, init_state=None,
               state_out=None):
    t, width = qkvg.shape
    h = RET_HEADS
    dk = width // (6 * h)
    dv = 2 * dk
    chunk = min(RET_CHUNK_ROWS, seq_len)
    n_chunks = seq_len // chunk
    x3 = qkvg.reshape(t // seq_len, seq_len, width)
    k_blk, v_blk, g_blk = h, (2 * h * dk) // dv, (2 * h * dk + h * dv) // dv
    use_rope = rope_tables is not None
    has_init = init_state is not None
    emit_state = state_out is not None

    in_specs = [
        pl.BlockSpec(memory_space=pltpu.SMEM),
        pl.BlockSpec((1, seq_len, dk), lambda b, hh: (b + seq_off, 0, hh)),
        pl.BlockSpec((1, seq_len, dk), lambda b, hh: (b + seq_off, 0, k_blk + hh)),
        pl.BlockSpec((1, seq_len, dv), lambda b, hh: (b + seq_off, 0, v_blk + hh)),
        pl.BlockSpec((1, seq_len, dv), lambda b, hh: (b + seq_off, 0, g_blk + hh)),
        pl.BlockSpec((1, 1, dv), lambda b, hh: (hh, 0, 0)),
    ]
    args = [log_g, x3, x3, x3, x3, gn_g.reshape(h, 1, dv)]
    if use_rope:
        in_specs += [pl.BlockSpec((seq_len, dk // 2), lambda b, hh: (0, 0))] * 2
        args += list(rope_tables)
    if has_init:
        layer = init_state[1]
        in_specs.append(pl.BlockSpec((1, 1, 2, 1, dk, dv), lambda b, hh: (b, layer, 0, hh, 0, 0)))
        args.append(init_state[0])
    out_specs = [pl.BlockSpec((1, seq_len, dv), lambda b, hh: (b, 0, hh))]
    out_shape = [jax.ShapeDtypeStruct((n_seq, seq_len, h * dv), BF16)]
    aliases = {}
    alias_state = zero_rest = False
    if emit_state:
        s_layer, n_layers, prev = state_out
        alias_state = prev is not None
        zero_rest = not alias_state and n_layers > 1
        if alias_state:
            aliases = {len(args): 1}
            in_specs.append(pl.BlockSpec(memory_space=pl.ANY))
            args.append(prev)
            out_specs.append(pl.BlockSpec((1, 1, 2, 1, dk, dv), lambda b, hh: (b, s_layer, 0, hh, 0, 0)))
        else:
            assert s_layer == 0
            out_specs.append(pl.BlockSpec((1, n_layers, 2, 1, dk, dv), lambda b, hh: (b, 0, 0, hh, 0, 0)))
        out_shape.append(jax.ShapeDtypeStruct((n_seq, n_layers, 2, h, dk, dv), F32))
    outs = pl.pallas_call(
        functools.partial(_retention_kernel, chunk=chunk, n_chunks=n_chunks, use_rope=use_rope,
                          has_init=has_init, emit_state=emit_state, alias_state=alias_state,
                          zero_rest=zero_rest, dk=dk),
        grid=(n_seq, h),
        in_specs=in_specs,
        out_specs=out_specs,
        out_shape=out_shape,
        scratch_shapes=[pltpu.VMEM((seq_len, dv), F32), pltpu.VMEM((dk, dv), F32)],
        input_output_aliases=aliases,
        compiler_params=_params(("parallel", "parallel")),
        name="retention_rope" if use_rope else "retention_ctx",
    )(*args)
    return outs if emit_state else outs[0]


def _rope_tables(seq_len, dk):
    rows = seq_len // GRID_W
    row = jnp.broadcast_to(jnp.arange(rows)[:, None], (rows, GRID_W)).reshape(-1).astype(F32)
    col = jnp.broadcast_to(jnp.arange(GRID_W)[None, :], (rows, GRID_W)).reshape(-1).astype(F32)
    n_freq = dk // 4
    inv = ROPE_BASE ** (-jnp.arange(n_freq, dtype=F32) / n_freq)
    ang = jnp.concatenate([row[:, None] * inv, col[:, None] * inv], axis=-1)
    return jnp.cos(ang), jnp.sin(ang)


def _proj_residual_kernel(*refs, gate_idx, seg_tiles, n_x):
    n_seg = len(seg_tiles)
    a_refs = refs[:n_seg]
    x_refs = refs[n_seg:n_seg + n_x]
    w_ref, mod_ref, o_ref = refs[n_seg + n_x:]
    gate = mod_ref[0][gate_idx:gate_idx + 1]

    def seg(s):
        x_ref = x_refs[s] if n_x > 1 else x_refs[0]
        acc = jnp.dot(a_refs[s][...], w_ref[0], preferred_element_type=F32)
        o_ref[...] = x_ref[...] + gate * acc

    _for_each_part(pl.program_id(1), seg_tiles, seg)


def _proj_residual(a_parts, w_stack, w_idx, x_parts, mod, *, gate_idx, t_ctx, l_lat):
    d = x_parts[0].shape[1]
    t = sum(p.shape[0] for p in a_parts)
    k = w_stack.shape[1]
    tm = _tile(min(t_ctx, l_lat), 512)
    tn = _tile(d, 1024)
    a_specs, seg_tiles = _part_specs(a_parts, tm, lambda j, i: i)
    assert sum(seg_tiles) * tm == t
    x_specs = []
    lo = 0
    for p in x_parts:
        n_tiles = p.shape[0] // tm
        x_specs.append(pl.BlockSpec(
            (tm, tn), lambda j, i, lo=lo, n_tiles=n_tiles: (jnp.clip(i - lo, 0, n_tiles - 1), j)))
        lo += n_tiles
    assert len(x_parts) == 1 or [p.shape[0] for p in x_parts] == [p.shape[0] for p in a_parts]
    return pl.pallas_call(
        functools.partial(_proj_residual_kernel, gate_idx=gate_idx, seg_tiles=seg_tiles, n_x=len(x_parts)),
        grid=(d // tn, t // tm),
        in_specs=a_specs + x_specs + [
            pl.BlockSpec((1, k, tn), lambda j, i: (w_idx, 0, j)),
            pl.BlockSpec((1, 6, tn), lambda j, i: (_mod_row(i * tm, t_ctx, l_lat), 0, j)),
        ],
        out_specs=pl.BlockSpec((tm, tn), lambda j, i: (i, j)),
        out_shape=jax.ShapeDtypeStruct((t, d), F32),
        compiler_params=_params(("parallel", "parallel")),
        name="proj_residual",
    )(*a_parts, *x_parts, w_stack, mod)


def _spatial_gate_kernel(u_ref, v_ref, lng_ref, lnb_ref, ws_ref, bs_ref, t_ref, vn_ref, *, n_sub, cw):
    v32 = v_ref[...].astype(F32)
    mu = jnp.mean(v32, axis=-1, keepdims=True)
    dlt = v32 - mu
    var = jnp.mean(dlt * dlt, axis=-1, keepdims=True)
    vn_ref[...] = (dlt * lax.rsqrt(var + NORM_EPS) * lng_ref[...] + lnb_ref[...]).astype(BF16)
    for n in range(n_sub):
        rs = slice(n * CM_CHUNK, (n + 1) * CM_CHUNK)
        for g in range(CM_GROUPS):
            cs = slice(g * cw, (g + 1) * cw)
            s = jnp.dot(ws_ref[g], vn_ref[rs, cs], preferred_element_type=F32) + bs_ref[g]
            t_ref[rs, cs] = (u_ref[rs, cs].astype(F32) * s).astype(t_ref.dtype)


def _spatial_gate(uv, ln_g, ln_b, w_s, b_s):
    t, two_cw = uv.shape
    width = two_cw // 2
    cw = width // CM_GROUPS
    tm = 2 * CM_CHUNK
    return pl.pallas_call(
        functools.partial(_spatial_gate_kernel, n_sub=tm // CM_CHUNK, cw=cw),
        grid=(t // tm,),
        in_specs=[
            pl.BlockSpec((tm, width), lambda i: (i, 0)),
            pl.BlockSpec((tm, width), lambda i: (i, 1)),
            pl.BlockSpec((1, width), lambda i: (0, 0)),
            pl.BlockSpec((1, width), lambda i: (0, 0)),
            pl.BlockSpec((CM_GROUPS, CM_CHUNK, CM_CHUNK), lambda i: (0, 0, 0)),
            pl.BlockSpec((CM_GROUPS, CM_CHUNK, 1), lambda i: (0, 0, 0)),
        ],
        out_specs=pl.BlockSpec((tm, width), lambda i: (i, 0)),
        out_shape=jax.ShapeDtypeStruct((t, width), BF16),
        scratch_shapes=[pltpu.VMEM((tm, width), BF16)],
        compiler_params=_params(("parallel",)),
        name="spatial_gate",
    )(uv, uv, ln_g.reshape(1, width), ln_b.reshape(1, width), w_s.astype(BF16),
      b_s.reshape(CM_GROUPS, CM_CHUNK, 1))


def _router_kernel(x_ref, g_ref, mod_ref, wr_ref, br_ref, h_ref, ids_ref, wts_ref, cnt_ref, run_ref):
    m = mod_ref[0]
    h = _norm_mod(x_ref[...], g_ref[...], m[3:4], m[4:5])
    hp = h.shape[1] // 2
    h_ref[...] = _pack_pair(h[:, :hp], h[:, hp:])
    h_hi = h.astype(BF16)
    h_lo = (h - h_hi.astype(F32)).astype(BF16)
    w = wr_ref[...]
    w_hi = w.astype(BF16)
    w_lo = (w - w_hi.astype(F32)).astype(BF16)
    lg = (jnp.dot(h_hi, w_hi, preferred_element_type=F32)
          + (jnp.dot(h_lo, w_hi, preferred_element_type=F32) + jnp.dot(h_hi, w_lo, preferred_element_type=F32)))
    lg = lg + br_ref[...]

    tm = lg.shape[0]
    lane = lax.broadcasted_iota(I32, lg.shape, 1).astype(F32)
    far = float(ROUTER_LANES)

    def first_max(vals):
        top = jnp.max(vals, axis=-1, keepdims=True)
        return top, jnp.min(jnp.where(vals == top, lane, far), axis=-1, keepdims=True)

    in_groups = lane < N_GROUPS
    g_top, grp = first_max(jnp.where(in_groups, lg, NEG_LOGIT))
    p_grp = 1.0 / jnp.sum(jnp.where(in_groups, jnp.exp(lg - g_top), 0.0), axis=-1, keepdims=True)
    lo = N_GROUPS + EXPERTS_PER_GROUP * grp
    cand = jnp.where(jnp.logical_and(lane >= lo, lane < lo + EXPERTS_PER_GROUP), lg, NEG_LOGIT)
    v1, i1 = first_max(cand)
    v2, i2 = first_max(jnp.where(lane == i1, NEG_LOGIT, cand))
    e2 = jnp.exp(v2 - v1)
    w1 = (1.0 / (1.0 + e2)) * p_grp
    w2 = (e2 / (1.0 + e2)) * p_grp

    @pl.when(pl.program_id(0) == 0)
    def _():
        run_ref[...] = jnp.zeros_like(run_ref)

    oh1 = (lane == i1).astype(F32)
    oh2 = (lane == i2).astype(F32)
    both = oh1 + oh2
    earlier = (lax.broadcasted_iota(I32, (tm, tm), 0) > lax.broadcasted_iota(I32, (tm, tm), 1))
    before = jnp.dot(earlier.astype(BF16), both.astype(BF16), preferred_element_type=F32) + run_ref[0:1, :]
    r1 = jnp.sum(oh1 * before, axis=-1, keepdims=True)
    r2 = jnp.sum(oh2 * before, axis=-1, keepdims=True)
    run_ref[...] = run_ref[...] + jnp.sum(both, axis=0, keepdims=True)
    cnt_ref[...] = run_ref[...].astype(I32)

    ids = jnp.where(lane == 0, i1 - N_GROUPS,
                    jnp.where(lane == 1, i2 - N_GROUPS, jnp.where(lane == 2, r1, jnp.where(lane == 3, r2, 0.0))))
    ids_ref[...] = ids.astype(I32)
    wts_ref[...] = jnp.where(lane == 0, w1, jnp.where(lane == 1, w2, 0.0))


def _router(x, gamma, mod, w_router_all, b_router_all, *, t_ctx, l_lat):
    t, d = x.shape
    tm = _tile(min(t_ctx, l_lat), 512)
    return pl.pallas_call(
        _router_kernel,
        grid=(t // tm,),
        in_specs=[
            pl.BlockSpec((tm, d), lambda i: (i, 0)),
            pl.BlockSpec((1, d), lambda i: (0, 0)),
            pl.BlockSpec((1, 6, d), lambda i: (_mod_row(i * tm, t_ctx, l_lat), 0, 0)),
            pl.BlockSpec((d, ROUTER_LANES), lambda i: (0, 0)),
            pl.BlockSpec((1, ROUTER_LANES), lambda i: (0, 0)),
        ],
        out_specs=[
            pl.BlockSpec((tm, d // 2), lambda i: (i, 0)),
            pl.BlockSpec((tm, ROUTER_LANES), lambda i: (i, 0)),
            pl.BlockSpec((tm, ROUTER_LANES), lambda i: (i, 0)),
            pl.BlockSpec((8, ROUTER_LANES), lambda i: (0, 0)),
        ],
        out_shape=[jax.ShapeDtypeStruct((t, d // 2), I32), jax.ShapeDtypeStruct((t, ROUTER_LANES), I32),
                   jax.ShapeDtypeStruct((t, ROUTER_LANES), F32), jax.ShapeDtypeStruct((8, ROUTER_LANES), I32)],
        scratch_shapes=[pltpu.VMEM((8, ROUTER_LANES), F32)],
        compiler_params=_params(("arbitrary",)),
        name="moe_router",
    )(x, gamma.reshape(1, d), mod, w_router_all, b_router_all)


def _routing_plan(ids, wts, cnt, n_blocks):
    expert_ids = ids[:, 0:TOP_K].reshape(-1)
    rank = ids[:, TOP_K:2 * TOP_K].reshape(-1)
    weights = wts[:, 0:TOP_K]
    counts = cnt[0, N_GROUPS:N_GROUPS + N_EXPERTS]
    e_range = jnp.arange(N_EXPERTS, dtype=I32)
    onehot = (expert_ids[:, None] == e_range[None, :]).astype(I32)
    padded = (counts + MOE_ROWS - 1) // MOE_ROWS * MOE_ROWS
    pad_ends = jnp.cumsum(padded)
    pad_starts = pad_ends - padded
    pos = (jnp.sum(onehot * pad_starts[None, :], axis=1) + rank).astype(I32)
    blk_start = jnp.arange(n_blocks, dtype=I32) * MOE_ROWS
    block_e = jnp.minimum(jnp.sum((pad_ends[None, :] <= blk_start[:, None]).astype(I32), axis=1),
                          N_EXPERTS - 1).astype(I32)
    n_used = (pad_ends[-1] // MOE_ROWS).astype(I32).reshape(1)
    later = jnp.where((counts[None, :] > 0) & (e_range[None, :] > e_range[:, None]), e_range[None, :], N_EXPERTS)
    next_of = jnp.min(later, axis=1)
    next_of = jnp.where(next_of == N_EXPERTS, -1, next_of).astype(I32)
    next_e = jnp.sum((block_e[:, None] == e_range[None, :]).astype(I32) * next_of[None, :], axis=1).astype(I32)
    pad_lo = (pad_starts + counts).astype(I32)
    return weights, pos, block_e, next_e, n_used, pad_lo, pad_ends.astype(I32)


def _dispatch_kernel(pos_ref, lo_ref, hi_ref, h_ref, xb_hbm, zero_ref, sem, zsem):
    i = pl.program_id(0)
    tm = h_ref.shape[0]
    base = i * (tm * TOP_K)

    def body(r, carry):
        for kk in range(TOP_K):
            p = pos_ref[base + TOP_K * r + kk]
            pltpu.make_async_copy(h_ref.at[pl.ds(r, 1), :], xb_hbm.at[pl.ds(p, 1), :], sem).start()
        return carry

    lax.fori_loop(0, tm, body, 0, unroll=8)
    for _ in range(TOP_K):
        pltpu.make_async_copy(h_ref, xb_hbm.at[pl.ds(0, tm), :], sem).wait()

    @pl.when(i == pl.num_programs(0) - 1)
    def _():
        zero_ref[...] = jnp.zeros_like(zero_ref)

        def zero_row(p):
            return pltpu.make_async_copy(zero_ref.at[pl.ds(0, 1), :], xb_hbm.at[pl.ds(p, 1), :], zsem)

        def per_expert(e, carry):
            lo = lo_ref[e]
            hi = hi_ref[e]

            def start(p, c):
                zero_row(p).start()
                return c

            def wait(p, c):
                zero_row(p).wait()
                return c

            lax.fori_loop(lo, hi, start, 0)
            lax.fori_loop(lo, hi, wait, 0)
            return carry

        lax.fori_loop(0, N_EXPERTS, per_expert, 0)

        n_blocks = xb_hbm.shape[0] // MOE_ROWS
        n_used = hi_ref[N_EXPERTS - 1] // MOE_ROWS

        def zero_block(blk):
            dst = xb_hbm.at[pl.ds(pl.multiple_of(blk * MOE_ROWS, MOE_ROWS), MOE_ROWS), :]
            return pltpu.make_async_copy(zero_ref, dst, zsem)

        def start_block(blk, c):
            zero_block(blk).start()
            return c

        def wait_block(blk, c):
            zero_block(blk).wait()
            return c

        lax.fori_loop(n_used, n_blocks, start_block, 0)
        lax.fori_loop(n_used, n_blocks, wait_block, 0)


def _dispatch(hp, pos, pad_lo, pad_hi, n_blocks):
    t, width = hp.shape
    tm = _tile(t, 512)
    grid_spec = pltpu.PrefetchScalarGridSpec(
        num_scalar_prefetch=3,
        grid=(t // tm,),
        in_specs=[pl.BlockSpec((tm, width), lambda i, pos, lo, hi: (i, 0))],
        out_specs=pl.BlockSpec(memory_space=pl.ANY),
        scratch_shapes=[pltpu.VMEM((MOE_ROWS, width), I32), pltpu.SemaphoreType.DMA(()),
                        pltpu.SemaphoreType.DMA(())],
    )
    return pl.pallas_call(
        _dispatch_kernel,
        grid_spec=grid_spec,
        out_shape=jax.ShapeDtypeStruct((n_blocks * MOE_ROWS, width), I32),
        compiler_params=_params(("arbitrary",)),
        name="moe_dispatch",
    )(pos, pad_lo, pad_hi, hp)


def _expert_kernel(be_ref, nx_ref, nu_ref, x_ref, wg_hbm, wu_hbm, wd_hbm, y_ref,
                   sg_ref, su_ref, sd_ref, wg_ref, wu_ref, wd_ref, sem, *, layer):
    i = pl.program_id(0)
    n_used = nu_ref[0]
    e = be_ref[i]
    changed = jnp.logical_or(i == 0, e != be_ref[jnp.maximum(i - 1, 0)])
    hp = x_ref.shape[1]

    def weight_copies(ex):
        return (pltpu.make_async_copy(wg_hbm.at[layer, ex], sg_ref, sem.at[0]),
                pltpu.make_async_copy(wu_hbm.at[layer, ex], su_ref, sem.at[1]),
                pltpu.make_async_copy(wd_hbm.at[layer, ex], sd_ref, sem.at[2]))

    @pl.when(i == 0)
    def _():
        for cp in weight_copies(e):
            cp.start()

    @pl.when(jnp.logical_and(changed, i < n_used))
    def _():
        for cp in weight_copies(e):
            cp.wait()
        for src, dst in ((sg_ref, wg_ref), (su_ref, wu_ref), (sd_ref, wd_ref)):
            rows = src.shape[0]
            step = 256

            def cast(ci, carry, src=src, dst=dst):
                sl = pl.ds(pl.multiple_of(ci * step, step), step)
                dst[sl, :] = src[sl, :].astype(BF16)
                return carry

            lax.fori_loop(0, rows // step, cast, 0)
        nxt = nx_ref[i]

        @pl.when(nxt >= 0)
        def _():
            for cp in weight_copies(nxt):
                cp.start()

    @pl.when(i < n_used)
    def _():
        xa, xb = _unpack_pair(x_ref[...])
        x = jnp.concatenate([xa.astype(BF16), xb.astype(BF16)], axis=1)
        hg = jnp.dot(x, wg_ref[...], preferred_element_type=F32)
        hu = jnp.dot(x, wu_ref[...], preferred_element_type=F32)
        hm = (jax.nn.silu(hg) * hu).astype(BF16)
        y = jnp.dot(hm, wd_ref[...], preferred_element_type=F32)
        y_ref[...] = _pack_pair(y[:, :hp], y[:, hp:])

    @pl.when(i >= n_used)
    def _():
        y_ref[...] = jnp.zeros_like(y_ref)


def _experts(xb, block_e, next_e, n_used, w_gate, w_up, w_down, layer, n_blocks):
    width = xb.shape[1]
    _, _, d, hid = w_gate.shape
    grid_spec = pltpu.PrefetchScalarGridSpec(
        num_scalar_prefetch=3,
        grid=(n_blocks,),
        in_specs=[
            pl.BlockSpec((MOE_ROWS, width), lambda i, be, nx, nu: (jnp.minimum(i, nu[0] - 1), 0)),
            pl.BlockSpec(memory_space=pl.ANY),
            pl.BlockSpec(memory_space=pl.ANY),
            pl.BlockSpec(memory_space=pl.ANY),
        ],
        out_specs=pl.BlockSpec((MOE_ROWS, width), lambda i, be, nx, nu: (i, 0)),
        scratch_shapes=[
            pltpu.VMEM((d, hid), F32), pltpu.VMEM((d, hid), F32), pltpu.VMEM((hid, d), F32),
            pltpu.VMEM((d, hid), BF16), pltpu.VMEM((d, hid), BF16), pltpu.VMEM((hid, d), BF16),
            pltpu.SemaphoreType.DMA((3,)),
        ],
    )
    return pl.pallas_call(
        functools.partial(_expert_kernel, layer=layer),
        grid_spec=grid_spec,
        out_shape=jax.ShapeDtypeStruct((n_blocks * MOE_ROWS, width), I32),
        compiler_params=_params(("arbitrary",)),
        name="moe_experts",
    )(block_e, next_e, n_used, xb, w_gate, w_up, w_down)


def _combine_kernel(pos_ref, x_ref, w_ref, mod_ref, fg_ref, yb_hbm, *rest, final_norm, seg_tiles):
    o_refs = rest[:len(seg_tiles)]
    ybuf, sem = rest[len(seg_tiles):]
    i = pl.program_id(0)
    n_tiles = pl.num_programs(0)
    tm = x_ref.shape[0]
    hp = ybuf.shape[2]

    def start_gather(tile, slot):
        base = tile * (tm * TOP_K)

        def body(r, carry):
            for kk in range(TOP_K):
                p = pos_ref[base + TOP_K * r + kk]
                pltpu.make_async_copy(yb_hbm.at[pl.ds(p, 1), :], ybuf.at[slot, pl.ds(kk * tm + r, 1), :],
                                      sem.at[slot]).start()
            return carry

        lax.fori_loop(0, tm, body, 0, unroll=8)

    def wait_gather(slot):
        pltpu.make_async_copy(yb_hbm.at[pl.ds(0, TOP_K * tm), :], ybuf.at[slot], sem.at[slot]).wait()

    slot = lax.rem(i, 2)

    @pl.when(i == 0)
    def _():
        start_gather(0, 0)

    @pl.when(i + 1 < n_tiles)
    def _():
        start_gather(i + 1, 1 - slot)

    wait_gather(slot)
    w = w_ref[...]
    a0, b0 = _unpack_pair(ybuf[slot, 0:tm, :])
    a1, b1 = _unpack_pair(ybuf[slot, tm:2 * tm, :])
    gate = mod_ref[0][5:6]
    out_a = x_ref[:, :hp] + gate[:, :hp] * (w[:, 0:1] * a0 + w[:, 1:2] * a1)
    out_b = x_ref[:, hp:] + gate[:, hp:] * (w[:, 0:1] * b0 + w[:, 1:2] * b1)
    if final_norm:
        ms = (jnp.sum(out_a * out_a, axis=-1, keepdims=True)
              + jnp.sum(out_b * out_b, axis=-1, keepdims=True)) / (2 * hp)
        inv = lax.rsqrt(ms + NORM_EPS)
        out_a = out_a * inv * fg_ref[:, :hp]
        out_b = out_b * inv * fg_ref[:, hp:]
    def store(s):
        o_refs[s][:, :hp] = out_a
        o_refs[s][:, hp:] = out_b

    _for_each_part(i, seg_tiles, store)


def _combine(x, yb, pos, weights, mod, final_g, *, final_norm, out_rows, t_ctx, l_lat):
    t, d = x.shape
    tm = _tile(min(t_ctx, l_lat), 256)
    seg_tiles = tuple(r // tm for r in out_rows)
    assert sum(seg_tiles) * tm == t
    out_specs, lo = [], 0
    for n_seg in seg_tiles:
        out_specs.append(pl.BlockSpec(
            (tm, d), lambda i, pos, lo=lo, n_seg=n_seg: (jnp.clip(i - lo, 0, n_seg - 1), 0)))
        lo += n_seg
    grid_spec = pltpu.PrefetchScalarGridSpec(
        num_scalar_prefetch=1,
        grid=(t // tm,),
        in_specs=[
            pl.BlockSpec((tm, d), lambda i, pos: (i, 0)),
            pl.BlockSpec((tm, TOP_K), lambda i, pos: (i, 0)),
            pl.BlockSpec((1, 6, d), lambda i, pos: (_mod_row(i * tm, t_ctx, l_lat), 0, 0)),
            pl.BlockSpec((1, d), lambda i, pos: (0, 0)),
            pl.BlockSpec(memory_space=pl.ANY),
        ],
        out_specs=out_specs,
        scratch_shapes=[pltpu.VMEM((2, TOP_K * tm, d // 2), I32), pltpu.SemaphoreType.DMA((2,))],
    )
    return pl.pallas_call(
        functools.partial(_combine_kernel, final_norm=final_norm, seg_tiles=seg_tiles),
        grid_spec=grid_spec,
        out_shape=[jax.ShapeDtypeStruct((r, d), F32) for r in out_rows],
        compiler_params=_params(("arbitrary",)),
        name="moe_combine",
    )(pos, x, weights, mod, final_g.reshape(1, d), yb)


def _moe(x, gamma, mod, layer, w_group, b_group, w_router, b_router, w_gate, w_up, w_down, final_g, *,
         final_norm, out_rows, t_ctx, l_lat):
    t, d = x.shape
    pad = ROUTER_LANES - N_GROUPS - N_EXPERTS
    w_all = jnp.concatenate([w_group[layer], w_router[layer], jnp.zeros((d, pad), F32)], axis=1)
    b_all = jnp.concatenate([b_group[layer], b_router[layer], jnp.zeros((pad,), F32)]).reshape(1, ROUTER_LANES)
    hp, ids, wts, cnt = _router(x, gamma, mod, w_all, b_all, t_ctx=t_ctx, l_lat=l_lat)
    n_blocks = (t * TOP_K) // MOE_ROWS + N_EXPERTS
    weights, pos, block_e, next_e, n_used, pad_lo, pad_hi = _routing_plan(ids, wts, cnt, n_blocks)
    xb = _dispatch(hp, pos, pad_lo, pad_hi, n_blocks)
    yb = _experts(xb, block_e, next_e, n_used, w_gate, w_up, w_down, layer, n_blocks)
    return _combine(x, yb, pos, weights, mod, final_g, final_norm=final_norm, out_rows=out_rows,
                    t_ctx=t_ctx, l_lat=l_lat)


def kernel(x_prompt, x_sample, c, state_ret, c_ctx, ada_w, ada_b, norm1_g, norm2_g, ret_w_qkvg, ret_w_o,
           ret_gn_g, ret_log_decay, cm_w_in, cm_ln_g, cm_ln_b, cm_w_s, cm_b_s, cm_w_out, moe_w_group,
           moe_b_group, moe_w_router, moe_b_router, moe_w_gate, moe_w_up, moe_w_down, final_norm_g):
    batch, seq, d = x_prompt.shape
    dec_batch, dec_seq, _ = x_sample.shape
    depth = ada_w.shape[0]
    n_ret = ret_w_qkvg.shape[0]
    t_ctx = batch * seq
    assert 1 + dec_batch <= COND_ROWS
    tiles = dict(t_ctx=t_ctx, l_lat=dec_seq)

    t_lat = dec_batch * dec_seq
    x_parts = [x_prompt.reshape(t_ctx, d), x_sample.reshape(t_lat, d)]
    cond = jnp.concatenate([c_ctx[None, :], c, jnp.zeros((COND_ROWS - 1 - dec_batch, d), F32)], axis=0)
    mod_all = _ada_modulation(cond, ada_w, ada_b).reshape(depth, COND_ROWS, 6, d)
    dk = d // RET_HEADS
    rope = _rope_tables(dec_seq, dk)
    w_qkvg, w_o = ret_w_qkvg.astype(BF16), ret_w_o.astype(BF16)
    w_in, w_out = cm_w_in.astype(BF16), cm_w_out.astype(BF16)

    states = None
    for layer in range(depth):
        mod = mod_all[layer]
        j = layer // 2
        last = layer == depth - 1
        if layer % 2 == 0:
            qkvg = _norm_matmul(x_parts, norm1_g[layer], mod, w_qkvg, j, shift_idx=0, scale_idx=1, act=None,
                                **tiles)
            log_g = -jnp.exp(ret_log_decay[j].astype(F32))
            o_ctx, states = _retention(qkvg, log_g, ret_gn_g[j], seq_len=seq, n_seq=batch, seq_off=0,
                                       state_out=(j, n_ret, states))
            o_lat = _retention(qkvg, log_g, ret_gn_g[j], seq_len=dec_seq, n_seq=dec_batch,
                               seq_off=t_ctx // dec_seq, rope_tables=rope, init_state=(state_ret, j))
            a_parts = [o_ctx.reshape(t_ctx, -1), o_lat.reshape(t_lat, -1)]
            x = _proj_residual(a_parts, w_o, j, x_parts, mod, gate_idx=2, **tiles)
        else:
            uv = _norm_matmul(x_parts, norm1_g[layer], mod, w_in, j, shift_idx=0, scale_idx=1, act="gelu",
                              **tiles)
            gated = _spatial_gate(uv, cm_ln_g[j], cm_ln_b[j], cm_w_s[j], cm_b_s[j])
            x = _proj_residual([gated], w_out, j, x_parts, mod, gate_idx=2, **tiles)
        x_parts = _moe(x, norm2_g[layer], mod, layer, moe_w_group, moe_b_group, moe_w_router, moe_b_router,
                       moe_w_gate, moe_w_up, moe_w_down, final_norm_g, final_norm=last,
                       out_rows=(t_ctx, t_lat) if last else (t_ctx + t_lat,), **tiles)

    y_prompt = x_parts[0].reshape(batch, seq, d)
    y_sample = x_parts[1].reshape(dec_batch, dec_seq, d)
    return (y_prompt, y_sample, states)
```

```python
import functools

import jax
import jax.numpy as jnp
from jax import lax
from jax.experimental import pallas as pl
from jax.experimental.pallas import tpu as pltpu

F32 = jnp.float32
BF16 = jnp.bfloat16
I32 = jnp.int32

RET_HEADS = 8
GRID_W = 64
ROPE_BASE = 10000.0
CM_CHUNK = 128
CM_GROUPS = 8
N_GROUPS = 4
EXPERTS_PER_GROUP = 4
N_EXPERTS = N_GROUPS * EXPERTS_PER_GROUP
TOP_K = 2
NORM_EPS = 1e-6

MOE_ROWS = 256
ROUTER_LANES = 128
COND_ROWS = 16
RET_CHUNK_ROWS = 256
RET_UNROLL = 4
V7X_VMEM_LIMIT = 56 * 1024 * 1024
HI_MASK = -65536
NEG_LOGIT = -3.0e38


def _params(semantics):
    return pltpu.CompilerParams(dimension_semantics=semantics, vmem_limit_bytes=V7X_VMEM_LIMIT)


def _tile(n, pref):
    t = min(n, pref)
    while n % t:
        t //= 2
    return t


def _pack_pair(a, b):
    a_bits = lax.bitcast_convert_type(a.astype(BF16).astype(F32), I32)
    b_bits = lax.bitcast_convert_type(b.astype(BF16).astype(F32), I32)
    return a_bits | lax.shift_right_logical(b_bits, 16)


def _unpack_pair(w):
    a = lax.bitcast_convert_type(w & HI_MASK, F32)
    b = lax.bitcast_convert_type(lax.shift_left(w, 16), F32)
    return a, b


def _ada_kernel(cond_ref, w_ref, b_ref, o_ref):
    s = jax.nn.silu(cond_ref[...])
    o_ref[0] = jnp.dot(s.astype(BF16), w_ref[0].astype(BF16), preferred_element_type=F32) + b_ref[0]


def _ada_modulation(cond, ada_w, ada_b):
    depth, d, n = ada_w.shape
    tn = _tile(n, 1024)
    return pl.pallas_call(
        _ada_kernel,
        grid=(depth, n // tn),
        in_specs=[
            pl.BlockSpec((COND_ROWS, d), lambda l, j: (0, 0)),
            pl.BlockSpec((1, d, tn), lambda l, j: (l, 0, j)),
            pl.BlockSpec((1, 1, tn), lambda l, j: (l, 0, j)),
        ],
        out_specs=pl.BlockSpec((1, COND_ROWS, tn), lambda l, j: (l, 0, j)),
        out_shape=jax.ShapeDtypeStruct((depth, COND_ROWS, n), F32),
        compiler_params=_params(("parallel", "parallel")),
        name="ada_modulation",
    )(cond, ada_w, ada_b.reshape(depth, 1, n))


def _norm_mod(x, gamma, shift, scale):
    y = x * lax.rsqrt(jnp.mean(x * x, axis=-1, keepdims=True) + NORM_EPS)
    return (y * gamma) * (1.0 + scale) + shift


def _mod_row(tile_start, t_ctx, l_lat):
    return jnp.where(tile_start < t_ctx, 0, 1 + jnp.maximum(tile_start - t_ctx, 0) // l_lat)


def _part_specs(parts, tm, index_of, single_buffer=False):
    specs, seg_tiles, lo = [], [], 0
    extra = dict(pipeline_mode=pl.Buffered(1)) if single_buffer and len(parts) > 1 else {}
    for p in parts:
        n_tiles = p.shape[0] // tm
        specs.append(pl.BlockSpec(
            (tm, p.shape[1]),
            lambda *ids, lo=lo, n_tiles=n_tiles: (jnp.clip(index_of(*ids) - lo, 0, n_tiles - 1), 0), **extra))
        seg_tiles.append(n_tiles)
        lo += n_tiles
    return specs, tuple(seg_tiles)


def _for_each_part(i, seg_tiles, fn):
    if len(seg_tiles) == 1:
        fn(0)
        return
    lo = 0
    for s, n_tiles in enumerate(seg_tiles):
        pl.when(jnp.logical_and(i >= lo, i < lo + n_tiles))(functools.partial(fn, s))
        lo += n_tiles


def _norm_matmul_kernel(*refs, shift_idx, scale_idx, act, seg_tiles):
    n_seg = len(seg_tiles)
    x_refs = refs[:n_seg]
    g_ref, mod_ref, w_ref, o_ref, h_ref = refs[n_seg:]

    @pl.when(pl.program_id(1) == 0)
    def _():
        def prologue(s):
            m = mod_ref[0]
            h = _norm_mod(x_refs[s][...], g_ref[...], m[shift_idx:shift_idx + 1], m[scale_idx:scale_idx + 1])
            h_ref[...] = h.astype(BF16)

        _for_each_part(pl.program_id(0), seg_tiles, prologue)

    acc = jnp.dot(h_ref[...], w_ref[0], preferred_element_type=F32)
    if act == "gelu":
        acc = jax.nn.gelu(acc)
    o_ref[...] = acc.astype(o_ref.dtype)


def _norm_matmul(x_parts, gamma, mod, w_stack, w_idx, *, shift_idx, scale_idx, act, t_ctx, l_lat):
    d = x_parts[0].shape[1]
    t = sum(p.shape[0] for p in x_parts)
    n = w_stack.shape[2]
    tm = _tile(min(t_ctx, l_lat), 1024)
    tn = _tile(n, 1024)
    x_specs, seg_tiles = _part_specs(x_parts, tm, lambda i, j: i, single_buffer=True)
    return pl.pallas_call(
        functools.partial(_norm_matmul_kernel, shift_idx=shift_idx, scale_idx=scale_idx, act=act,
                          seg_tiles=seg_tiles),
        grid=(t // tm, n // tn),
        in_specs=x_specs + [
            pl.BlockSpec((1, d), lambda i, j: (0, 0)),
            pl.BlockSpec((1, 6, d), lambda i, j: (_mod_row(i * tm, t_ctx, l_lat), 0, 0)),
            pl.BlockSpec((1, d, tn), lambda i, j: (w_idx, 0, j)),
        ],
        out_specs=pl.BlockSpec((tm, tn), lambda i, j: (i, j)),
        out_shape=jax.ShapeDtypeStruct((t, n), BF16),
        scratch_shapes=[pltpu.VMEM((tm, d), BF16)],
        compiler_params=_params(("parallel", "arbitrary")),
        name="norm_matmul_" + str(act),
    )(*x_parts, gamma.reshape(1, d), mod, w_stack)


def _retention_kernel(*refs, chunk, n_chunks, hps, use_rope, has_init, emit_state, alias_state, zero_rest, dk):
    it = iter(refs)
    lg_ref, q_ref, k_ref, v_ref, g_ref, gn_ref = (next(it) for _ in range(6))
    cos_ref = sin_ref = s0_ref = st_ref = None
    if use_rope:
        cos_ref, sin_ref = next(it), next(it)
    if has_init:
        s0_ref = next(it)
    if alias_state:
        next(it)
    o_ref = next(it)
    if emit_state:
        st_ref = next(it)
    kr_ref, kf_ref, sb_ref, s_ref = (next(it) for _ in range(4))

    dv = 2 * dk
    c = chunk
    half = dk // 2
    k_scale = dk ** -0.5
    pos = lax.broadcasted_iota(jnp.int32, (c, 1), 0).astype(F32)
    rel = (lax.broadcasted_iota(jnp.int32, (c, c), 0) - lax.broadcasted_iota(jnp.int32, (c, c), 1)).astype(F32)
    tn_dims = (((0,), (0,)), ((), ()))
    nt_dims = (((1,), (1,)), ((), ()))

    def decays(hd):
        head = pl.program_id(1) * hps + hd
        lg_f = lg_ref[0, head]
        lg_b = lg_ref[1, head]
        mask = (jnp.where(rel >= 0, jnp.exp(lg_f * jnp.maximum(rel, 0.0)), 0.0)
                + jnp.where(rel <= 0, jnp.exp(lg_b * jnp.maximum(-rel, 0.0)), 0.0))
        return dict(
            qd_f=jnp.exp(lg_f * (pos + 1.0)), kd_f=jnp.exp(lg_f * (c - 1.0 - pos)) * k_scale,
            qd_b=jnp.exp(lg_b * (c - pos)), kd_b=jnp.exp(lg_b * pos) * k_scale,
            cd_f=jnp.exp(jnp.full((1, 1), c, F32) * lg_f), cd_b=jnp.exp(jnp.full((1, 1), c, F32) * lg_b),
            mask=mask * k_scale)

    dec = [decays(hd) for hd in range(hps)]

    def rows(ci):
        if isinstance(ci, int):
            return pl.ds(ci * c, c)
        return pl.ds(pl.multiple_of(ci * c, c), c)

    def kcols(hd):
        return slice(hd * dk, (hd + 1) * dk)

    def vcols(hd):
        return slice(hd * dv, (hd + 1) * dv)

    def rotated(ref, sl, hd):
        x = ref[0, sl, kcols(hd)].astype(F32)
        if use_rope:
            cos = cos_ref[sl, :]
            sin = sin_ref[sl, :]
            x1, x2 = x[:, :half], x[:, half:]
            x = jnp.concatenate([x1 * cos - x2 * sin, x1 * sin + x2 * cos], axis=-1)
        return x

    def emit(direction, hd, value):
        st_ref[0, 0, direction, hd] = value
        if zero_rest:
            for other in range(1, st_ref.shape[1]):
                st_ref[0, other, direction, hd] = jnp.zeros_like(value)

    def finish(sl, hd, o):
        mu = jnp.mean(o, axis=-1, keepdims=True)
        dlt = o - mu
        var = jnp.mean(dlt * dlt, axis=-1, keepdims=True)
        on = dlt * lax.rsqrt(var + NORM_EPS)
        gate = g_ref[0, sl, vcols(hd)].astype(F32)
        o_ref[0, sl, vcols(hd)] = ((on * gn_ref[hd]) * jax.nn.silu(gate)).astype(o_ref.dtype)

    def scores(q, kb, hd):
        a = lax.dot_general(q.astype(BF16), kb, nt_dims, preferred_element_type=F32) * dec[hd]["mask"]
        return a.astype(BF16)

    if n_chunks == 1 and not has_init:
        sl = rows(0)
        for hd in range(hps):
            q = rotated(q_ref, sl, hd)
            k = rotated(k_ref, sl, hd)
            v = v_ref[0, sl, vcols(hd)]
            finish(sl, hd, jnp.dot(scores(q, k.astype(BF16), hd), v, preferred_element_type=F32))
            if emit_state:
                for direction, kd in ((0, dec[hd]["kd_f"]), (1, dec[hd]["kd_b"])):
                    emit(direction, hd,
                         lax.dot_general((k * kd).astype(BF16), v, tn_dims, preferred_element_type=F32))
        return

    def init_states(direction):
        for hd in range(hps):
            if has_init:
                s_ref[hd] = s0_ref[0, 0, direction, hd]
            else:
                s_ref[hd] = jnp.zeros(s_ref.shape[1:], F32)

    init_states(1)

    def bwd_body(t, carry):
        ci = n_chunks - 1 - t
        sl = rows(ci)
        for hd in range(hps):
            k = rotated(k_ref, sl, hd)
            kr_ref[sl, kcols(hd)] = k.astype(BF16)
            kf_ref[sl, kcols(hd)] = (k * dec[hd]["kd_f"]).astype(BF16)
            sb_ref[ci * hps + hd] = s_ref[hd].astype(BF16)
            kv = lax.dot_general((k * dec[hd]["kd_b"]).astype(BF16), v_ref[0, sl, vcols(hd)], tn_dims,
                                 preferred_element_type=F32)
            s_ref[hd] = dec[hd]["cd_b"] * s_ref[hd] + kv
        return carry

    lax.fori_loop(0, n_chunks, bwd_body, 0, unroll=RET_UNROLL)
    if emit_state:
        for hd in range(hps):
            emit(1, hd, s_ref[hd])

    init_states(0)

    def fwd_body(ci, carry):
        sl = rows(ci)
        for hd in range(hps):
            q = rotated(q_ref, sl, hd)
            v = v_ref[0, sl, vcols(hd)]
            o = (jnp.dot(scores(q, kr_ref[sl, kcols(hd)], hd), v, preferred_element_type=F32)
                 + jnp.dot((q * dec[hd]["qd_f"]).astype(BF16), s_ref[hd].astype(BF16),
                           preferred_element_type=F32)
                 + jnp.dot((q * dec[hd]["qd_b"]).astype(BF16), sb_ref[ci * hps + hd],
                           preferred_element_type=F32))
            finish(sl, hd, o)
            kv = lax.dot_general(kf_ref[sl, kcols(hd)], v, tn_dims, preferred_element_type=F32)
            s_ref[hd] = dec[hd]["cd_f"] * s_ref[hd] + kv
        return carry

    lax.fori_loop(0, n_chunks, fwd_body, 0, unroll=RET_UNROLL)
    if emit_state:
        for hd in range(hps):
            emit(0, hd, s_ref[hd])


def _retention(qkvg, log_g, gn_g, *, seq_len, n_seq, seq_off, heads_per_step, rope_tables=None,
               init_state=None, state_out=None):
    t, width = qkvg.shape
    h = RET_HEADS
    hps = heads_per_step
    dk = width // (6 * h)
    dv = 2 * dk
    chunk = min(RET_CHUNK_ROWS, seq_len)
    n_chunks = seq_len // chunk
    x3 = qkvg.reshape(t // seq_len, seq_len, width)
    k_blk, v_blk, g_blk = h // hps, (2 * h * dk) // (hps * dv), (2 * h * dk + h * dv) // (hps * dv)
    use_rope = rope_tables is not None
    has_init = init_state is not None
    emit_state = state_out is not None

    in_specs = [
        pl.BlockSpec(memory_space=pltpu.SMEM),
        pl.BlockSpec((1, seq_len, hps * dk), lambda b, hh: (b + seq_off, 0, hh)),
        pl.BlockSpec((1, seq_len, hps * dk), lambda b, hh: (b + seq_off, 0, k_blk + hh)),
        pl.BlockSpec((1, seq_len, hps * dv), lambda b, hh: (b + seq_off, 0, v_blk + hh)),
        pl.BlockSpec((1, seq_len, hps * dv), lambda b, hh: (b + seq_off, 0, g_blk + hh)),
        pl.BlockSpec((hps, 1, dv), lambda b, hh: (hh, 0, 0)),
    ]
    args = [log_g, x3, x3, x3, x3, gn_g.reshape(h, 1, dv)]
    if use_rope:
        in_specs += [pl.BlockSpec((seq_len, dk // 2), lambda b, hh: (0, 0))] * 2
        args += list(rope_tables)
    if has_init:
        layer = init_state[1]
        in_specs.append(pl.BlockSpec((1, 1, 2, hps, dk, dv), lambda b, hh: (b, layer, 0, hh, 0, 0)))
        args.append(init_state[0])
    out_specs = [pl.BlockSpec((1, seq_len, hps * dv), lambda b, hh: (b, 0, hh))]
    out_shape = [jax.ShapeDtypeStruct((n_seq, seq_len, h * dv), BF16)]
    aliases = {}
    alias_state = zero_rest = False
    if emit_state:
        s_layer, n_layers, prev = state_out
        alias_state = prev is not None
        zero_rest = not alias_state and n_layers > 1
        if alias_state:
            aliases = {len(args): 1}
            in_specs.append(pl.BlockSpec(memory_space=pl.ANY))
            args.append(prev)
            out_specs.append(pl.BlockSpec((1, 1, 2, hps, dk, dv), lambda b, hh: (b, s_layer, 0, hh, 0, 0)))
        else:
            assert s_layer == 0
            out_specs.append(pl.BlockSpec((1, n_layers, 2, hps, dk, dv), lambda b, hh: (b, 0, 0, hh, 0, 0)))
        out_shape.append(jax.ShapeDtypeStruct((n_seq, n_layers, 2, h, dk, dv), F32))
    outs = pl.pallas_call(
        functools.partial(_retention_kernel, chunk=chunk, n_chunks=n_chunks, hps=hps, use_rope=use_rope,
                          has_init=has_init, emit_state=emit_state, alias_state=alias_state,
                          zero_rest=zero_rest, dk=dk),
        grid=(n_seq, h // hps),
        in_specs=in_specs,
        out_specs=out_specs,
        out_shape=out_shape,
        scratch_shapes=[pltpu.VMEM((seq_len, hps * dk), BF16), pltpu.VMEM((seq_len, hps * dk), BF16),
                        pltpu.VMEM((n_chunks * hps, dk, dv), BF16), pltpu.VMEM((hps, dk, dv), F32)],
        input_output_aliases=aliases,
        compiler_params=_params(("parallel", "parallel")),
        name="retention_rope" if use_rope else "retention_ctx",
    )(*args)
    return outs if emit_state else outs[0]


def _rope_tables(seq_len, dk):
    rows = seq_len // GRID_W
    row = jnp.broadcast_to(jnp.arange(rows)[:, None], (rows, GRID_W)).reshape(-1).astype(F32)
    col = jnp.broadcast_to(jnp.arange(GRID_W)[None, :], (rows, GRID_W)).reshape(-1).astype(F32)
    n_freq = dk // 4
    inv = ROPE_BASE ** (-jnp.arange(n_freq, dtype=F32) / n_freq)
    ang = jnp.concatenate([row[:, None] * inv, col[:, None] * inv], axis=-1)
    return jnp.cos(ang), jnp.sin(ang)


def _proj_residual_kernel(*refs, gate_idx, seg_tiles, n_x):
    n_seg = len(seg_tiles)
    a_refs = refs[:n_seg]
    x_refs = refs[n_seg:n_seg + n_x]
    w_ref, mod_ref, o_ref = refs[n_seg + n_x:]
    gate = mod_ref[0][gate_idx:gate_idx + 1]

    def seg(s):
        x_ref = x_refs[s] if n_x > 1 else x_refs[0]
        acc = jnp.dot(a_refs[s][...], w_ref[0], preferred_element_type=F32)
        o_ref[...] = x_ref[...] + gate * acc

    _for_each_part(pl.program_id(1), seg_tiles, seg)


def _proj_residual(a_parts, w_stack, w_idx, x_parts, mod, *, gate_idx, t_ctx, l_lat):
    d = x_parts[0].shape[1]
    t = sum(p.shape[0] for p in a_parts)
    k = w_stack.shape[1]
    tm = _tile(min(t_ctx, l_lat), 512)
    tn = _tile(d, 1024)
    a_specs, seg_tiles = _part_specs(a_parts, tm, lambda j, i: i)
    assert sum(seg_tiles) * tm == t
    x_specs = []
    lo = 0
    for p in x_parts:
        n_tiles = p.shape[0] // tm
        x_specs.append(pl.BlockSpec(
            (tm, tn), lambda j, i, lo=lo, n_tiles=n_tiles: (jnp.clip(i - lo, 0, n_tiles - 1), j)))
        lo += n_tiles
    assert len(x_parts) == 1 or [p.shape[0] for p in x_parts] == [p.shape[0] for p in a_parts]
    return pl.pallas_call(
        functools.partial(_proj_residual_kernel, gate_idx=gate_idx, seg_tiles=seg_tiles, n_x=len(x_parts)),
        grid=(d // tn, t // tm),
        in_specs=a_specs + x_specs + [
            pl.BlockSpec((1, k, tn), lambda j, i: (w_idx, 0, j)),
            pl.BlockSpec((1, 6, tn), lambda j, i: (_mod_row(i * tm, t_ctx, l_lat), 0, j)),
        ],
        out_specs=pl.BlockSpec((tm, tn), lambda j, i: (i, j)),
        out_shape=jax.ShapeDtypeStruct((t, d), F32),
        compiler_params=_params(("parallel", "parallel")),
        name="proj_residual",
    )(*a_parts, *x_parts, w_stack, mod)


def _spatial_gate_kernel(u_ref, v_ref, lng_ref, lnb_ref, ws_ref, bs_ref, t_ref, vn_ref, *, n_sub, cw):
    v32 = v_ref[...].astype(F32)
    mu = jnp.mean(v32, axis=-1, keepdims=True)
    dlt = v32 - mu
    var = jnp.mean(dlt * dlt, axis=-1, keepdims=True)
    vn_ref[...] = (dlt * lax.rsqrt(var + NORM_EPS) * lng_ref[...] + lnb_ref[...]).astype(BF16)
    for n in range(n_sub):
        rs = slice(n * CM_CHUNK, (n + 1) * CM_CHUNK)
        for g in range(CM_GROUPS):
            cs = slice(g * cw, (g + 1) * cw)
            s = jnp.dot(ws_ref[g], vn_ref[rs, cs], preferred_element_type=F32) + bs_ref[g]
            t_ref[rs, cs] = (u_ref[rs, cs].astype(F32) * s).astype(t_ref.dtype)


def _spatial_gate(uv, ln_g, ln_b, w_s, b_s):
    t, two_cw = uv.shape
    width = two_cw // 2
    cw = width // CM_GROUPS
    tm = 2 * CM_CHUNK
    return pl.pallas_call(
        functools.partial(_spatial_gate_kernel, n_sub=tm // CM_CHUNK, cw=cw),
        grid=(t // tm,),
        in_specs=[
            pl.BlockSpec((tm, width), lambda i: (i, 0)),
            pl.BlockSpec((tm, width), lambda i: (i, 1)),
            pl.BlockSpec((1, width), lambda i: (0, 0)),
            pl.BlockSpec((1, width), lambda i: (0, 0)),
            pl.BlockSpec((CM_GROUPS, CM_CHUNK, CM_CHUNK), lambda i: (0, 0, 0)),
            pl.BlockSpec((CM_GROUPS, CM_CHUNK, 1), lambda i: (0, 0, 0)),
        ],
        out_specs=pl.BlockSpec((tm, width), lambda i: (i, 0)),
        out_shape=jax.ShapeDtypeStruct((t, width), BF16),
        scratch_shapes=[pltpu.VMEM((tm, width), BF16)],
        compiler_params=_params(("parallel",)),
        name="spatial_gate",
    )(uv, uv, ln_g.reshape(1, width), ln_b.reshape(1, width), w_s.astype(BF16),
      b_s.reshape(CM_GROUPS, CM_CHUNK, 1))


def _router_kernel(x_ref, g_ref, mod_ref, wr_ref, br_ref, h_ref, ids_ref, wts_ref, cnt_ref, run_ref):
    m = mod_ref[0]
    h = _norm_mod(x_ref[...], g_ref[...], m[3:4], m[4:5])
    hp = h.shape[1] // 2
    h_ref[...] = _pack_pair(h[:, :hp], h[:, hp:])
    h_hi = h.astype(BF16)
    h_lo = (h - h_hi.astype(F32)).astype(BF16)
    w = wr_ref[...]
    w_hi = w.astype(BF16)
    w_lo = (w - w_hi.astype(F32)).astype(BF16)
    lg = (jnp.dot(h_hi, w_hi, preferred_element_type=F32)
          + (jnp.dot(h_lo, w_hi, preferred_element_type=F32) + jnp.dot(h_hi, w_lo, preferred_element_type=F32)))
    lg = lg + br_ref[...]

    tm = lg.shape[0]
    lane = lax.broadcasted_iota(I32, lg.shape, 1).astype(F32)
    far = float(ROUTER_LANES)

    def first_max(vals):
        top = jnp.max(vals, axis=-1, keepdims=True)
        return top, jnp.min(jnp.where(vals == top, lane, far), axis=-1, keepdims=True)

    in_groups = lane < N_GROUPS
    g_top, grp = first_max(jnp.where(in_groups, lg, NEG_LOGIT))
    p_grp = 1.0 / jnp.sum(jnp.where(in_groups, jnp.exp(lg - g_top), 0.0), axis=-1, keepdims=True)
    lo = N_GROUPS + EXPERTS_PER_GROUP * grp
    cand = jnp.where(jnp.logical_and(lane >= lo, lane < lo + EXPERTS_PER_GROUP), lg, NEG_LOGIT)
    v1, i1 = first_max(cand)
    v2, i2 = first_max(jnp.where(lane == i1, NEG_LOGIT, cand))
    e2 = jnp.exp(v2 - v1)
    w1 = (1.0 / (1.0 + e2)) * p_grp
    w2 = (e2 / (1.0 + e2)) * p_grp

    @pl.when(pl.program_id(0) == 0)
    def _():
        run_ref[...] = jnp.zeros_like(run_ref)

    oh1 = (lane == i1).astype(F32)
    oh2 = (lane == i2).astype(F32)
    both = oh1 + oh2
    earlier = (lax.broadcasted_iota(I32, (tm, tm), 0) > lax.broadcasted_iota(I32, (tm, tm), 1))
    before = jnp.dot(earlier.astype(BF16), both.astype(BF16), preferred_element_type=F32) + run_ref[0:1, :]
    r1 = jnp.sum(oh1 * before, axis=-1, keepdims=True)
    r2 = jnp.sum(oh2 * before, axis=-1, keepdims=True)
    run_ref[...] = run_ref[...] + jnp.sum(both, axis=0, keepdims=True)
    cnt_ref[...] = run_ref[...].astype(I32)

    ids = jnp.where(lane == 0, i1 - N_GROUPS,
                    jnp.where(lane == 1, i2 - N_GROUPS, jnp.where(lane == 2, r1, jnp.where(lane == 3, r2, 0.0))))
    ids_ref[...] = ids.astype(I32)
    wts_ref[...] = jnp.where(lane == 0, w1, jnp.where(lane == 1, w2, 0.0))


def _router(x, gamma, mod, w_router_all, b_router_all, *, t_ctx, l_lat):
    t, d = x.shape
    tm = _tile(min(t_ctx, l_lat), 512)
    return pl.pallas_call(
        _router_kernel,
        grid=(t // tm,),
        in_specs=[
            pl.BlockSpec((tm, d), lambda i: (i, 0)),
            pl.BlockSpec((1, d), lambda i: (0, 0)),
            pl.BlockSpec((1, 6, d), lambda i: (_mod_row(i * tm, t_ctx, l_lat), 0, 0)),
            pl.BlockSpec((d, ROUTER_LANES), lambda i: (0, 0)),
            pl.BlockSpec((1, ROUTER_LANES), lambda i: (0, 0)),
        ],
        out_specs=[
            pl.BlockSpec((tm, d // 2), lambda i: (i, 0)),
            pl.BlockSpec((tm, ROUTER_LANES), lambda i: (i, 0)),
            pl.BlockSpec((tm, ROUTER_LANES), lambda i: (i, 0)),
            pl.BlockSpec((8, ROUTER_LANES), lambda i: (0, 0)),
        ],
        out_shape=[jax.ShapeDtypeStruct((t, d // 2), I32), jax.ShapeDtypeStruct((t, ROUTER_LANES), I32),
                   jax.ShapeDtypeStruct((t, ROUTER_LANES), F32), jax.ShapeDtypeStruct((8, ROUTER_LANES), I32)],
        scratch_shapes=[pltpu.VMEM((8, ROUTER_LANES), F32)],
        compiler_params=_params(("arbitrary",)),
        name="moe_router",
    )(x, gamma.reshape(1, d), mod, w_router_all, b_router_all)


def _routing_plan(ids, wts, cnt, n_blocks):
    expert_ids = ids[:, 0:TOP_K].reshape(-1)
    rank = ids[:, TOP_K:2 * TOP_K].reshape(-1)
    weights = wts[:, 0:TOP_K]
    counts = cnt[0, N_GROUPS:N_GROUPS + N_EXPERTS]
    e_range = jnp.arange(N_EXPERTS, dtype=I32)
    onehot = (expert_ids[:, None] == e_range[None, :]).astype(I32)
    padded = (counts + MOE_ROWS - 1) // MOE_ROWS * MOE_ROWS
    pad_ends = jnp.cumsum(padded)
    pad_starts = pad_ends - padded
    pos = (jnp.sum(onehot * pad_starts[None, :], axis=1) + rank).astype(I32)
    blk_start = jnp.arange(n_blocks, dtype=I32) * MOE_ROWS
    block_e = jnp.minimum(jnp.sum((pad_ends[None, :] <= blk_start[:, None]).astype(I32), axis=1),
                          N_EXPERTS - 1).astype(I32)
    n_used = (pad_ends[-1] // MOE_ROWS).astype(I32).reshape(1)
    later = jnp.where((counts[None, :] > 0) & (e_range[None, :] > e_range[:, None]), e_range[None, :], N_EXPERTS)
    next_of = jnp.min(later, axis=1)
    next_of = jnp.where(next_of == N_EXPERTS, -1, next_of).astype(I32)
    next_e = jnp.sum((block_e[:, None] == e_range[None, :]).astype(I32) * next_of[None, :], axis=1).astype(I32)
    pad_lo = (pad_starts + counts).astype(I32)
    return weights, pos, block_e, next_e, n_used, pad_lo, pad_ends.astype(I32)


def _dispatch_kernel(pos_ref, lo_ref, hi_ref, h_ref, xb_hbm, zero_ref, sem, zsem):
    i = pl.program_id(0)
    tm = h_ref.shape[0]
    base = i * (tm * TOP_K)

    def body(r, carry):
        for kk in range(TOP_K):
            p = pos_ref[base + TOP_K * r + kk]
            pltpu.make_async_copy(h_ref.at[pl.ds(r, 1), :], xb_hbm.at[pl.ds(p, 1), :], sem).start()
        return carry

    lax.fori_loop(0, tm, body, 0, unroll=8)
    for _ in range(TOP_K):
        pltpu.make_async_copy(h_ref, xb_hbm.at[pl.ds(0, tm), :], sem).wait()

    @pl.when(i == pl.num_programs(0) - 1)
    def _():
        zero_ref[...] = jnp.zeros_like(zero_ref)

        def zero_row(p):
            return pltpu.make_async_copy(zero_ref.at[pl.ds(0, 1), :], xb_hbm.at[pl.ds(p, 1), :], zsem)

        def per_expert(e, carry):
            lo = lo_ref[e]
            hi = hi_ref[e]

            def start(p, c):
                zero_row(p).start()
                return c

            def wait(p, c):
                zero_row(p).wait()
                return c

            lax.fori_loop(lo, hi, start, 0)
            lax.fori_loop(lo, hi, wait, 0)
            return carry

        lax.fori_loop(0, N_EXPERTS, per_expert, 0)

        n_blocks = xb_hbm.shape[0] // MOE_ROWS
        n_used = hi_ref[N_EXPERTS - 1] // MOE_ROWS

        def zero_block(blk):
            dst = xb_hbm.at[pl.ds(pl.multiple_of(blk * MOE_ROWS, MOE_ROWS), MOE_ROWS), :]
            return pltpu.make_async_copy(zero_ref, dst, zsem)

        def start_block(blk, c):
            zero_block(blk).start()
            return c

        def wait_block(blk, c):
            zero_block(blk).wait()
            return c

        lax.fori_loop(n_used, n_blocks, start_block, 0)
        lax.fori_loop(n_used, n_blocks, wait_block, 0)


def _dispatch(hp, pos, pad_lo, pad_hi, n_blocks):
    t, width = hp.shape
    tm = _tile(t, 512)
    grid_spec = pltpu.PrefetchScalarGridSpec(
        num_scalar_prefetch=3,
        grid=(t // tm,),
        in_specs=[pl.BlockSpec((tm, width), lambda i, pos, lo, hi: (i, 0))],
        out_specs=pl.BlockSpec(memory_space=pl.ANY),
        scratch_shapes=[pltpu.VMEM((MOE_ROWS, width), I32), pltpu.SemaphoreType.DMA(()),
                        pltpu.SemaphoreType.DMA(())],
    )
    return pl.pallas_call(
        _dispatch_kernel,
        grid_spec=grid_spec,
        out_shape=jax.ShapeDtypeStruct((n_blocks * MOE_ROWS, width), I32),
        compiler_params=_params(("arbitrary",)),
        name="moe_dispatch",
    )(pos, pad_lo, pad_hi, hp)


def _expert_kernel(be_ref, nx_ref, nu_ref, x_ref, wg_hbm, wu_hbm, wd_hbm, y_ref,
                   sg_ref, su_ref, sd_ref, wg_ref, wu_ref, wd_ref, sem, *, layer):
    i = pl.program_id(0)
    n_used = nu_ref[0]
    e = be_ref[i]
    changed = jnp.logical_or(i == 0, e != be_ref[jnp.maximum(i - 1, 0)])
    hp = x_ref.shape[1]

    def weight_copies(ex):
        return (pltpu.make_async_copy(wg_hbm.at[layer, ex], sg_ref, sem.at[0]),
                pltpu.make_async_copy(wu_hbm.at[layer, ex], su_ref, sem.at[1]),
                pltpu.make_async_copy(wd_hbm.at[layer, ex], sd_ref, sem.at[2]))

    @pl.when(i == 0)
    def _():
        for cp in weight_copies(e):
            cp.start()

    @pl.when(jnp.logical_and(changed, i < n_used))
    def _():
        for cp in weight_copies(e):
            cp.wait()
        for src, dst in ((sg_ref, wg_ref), (su_ref, wu_ref), (sd_ref, wd_ref)):
            rows = src.shape[0]
            step = 256

            def cast(ci, carry, src=src, dst=dst):
                sl = pl.ds(pl.multiple_of(ci * step, step), step)
                dst[sl, :] = src[sl, :].astype(BF16)
                return carry

            lax.fori_loop(0, rows // step, cast, 0)
        nxt = nx_ref[i]

        @pl.when(nxt >= 0)
        def _():
            for cp in weight_copies(nxt):
                cp.start()

    @pl.when(i < n_used)
    def _():
        xa, xb = _unpack_pair(x_ref[...])
        x = jnp.concatenate([xa.astype(BF16), xb.astype(BF16)], axis=1)
        hg = jnp.dot(x, wg_ref[...], preferred_element_type=F32)
        hu = jnp.dot(x, wu_ref[...], preferred_element_type=F32)
        hm = (jax.nn.silu(hg) * hu).astype(BF16)
        y = jnp.dot(hm, wd_ref[...], preferred_element_type=F32)
        y_ref[...] = _pack_pair(y[:, :hp], y[:, hp:])

    @pl.when(i >= n_used)
    def _():
        y_ref[...] = jnp.zeros_like(y_ref)


def _experts(xb, block_e, next_e, n_used, w_gate, w_up, w_down, layer, n_blocks):
    width = xb.shape[1]
    _, _, d, hid = w_gate.shape
    grid_spec = pltpu.PrefetchScalarGridSpec(
        num_scalar_prefetch=3,
        grid=(n_blocks,),
        in_specs=[
            pl.BlockSpec((MOE_ROWS, width), lambda i, be, nx, nu: (jnp.minimum(i, nu[0] - 1), 0)),
            pl.BlockSpec(memory_space=pl.ANY),
            pl.BlockSpec(memory_space=pl.ANY),
            pl.BlockSpec(memory_space=pl.ANY),
        ],
        out_specs=pl.BlockSpec((MOE_ROWS, width), lambda i, be, nx, nu: (i, 0)),
        scratch_shapes=[
            pltpu.VMEM((d, hid), F32), pltpu.VMEM((d, hid), F32), pltpu.VMEM((hid, d), F32),
            pltpu.VMEM((d, hid), BF16), pltpu.VMEM((d, hid), BF16), pltpu.VMEM((hid, d), BF16),
            pltpu.SemaphoreType.DMA((3,)),
        ],
    )
    return pl.pallas_call(
        functools.partial(_expert_kernel, layer=layer),
        grid_spec=grid_spec,
        out_shape=jax.ShapeDtypeStruct((n_blocks * MOE_ROWS, width), I32),
        compiler_params=_params(("arbitrary",)),
        name="moe_experts",
    )(block_e, next_e, n_used, xb, w_gate, w_up, w_down)


def _combine_kernel(pos_ref, x_ref, w_ref, mod_ref, fg_ref, yb_hbm, *rest, final_norm, seg_tiles):
    o_refs = rest[:len(seg_tiles)]
    ybuf, sem = rest[len(seg_tiles):]
    i = pl.program_id(0)
    n_tiles = pl.num_programs(0)
    tm = x_ref.shape[0]
    hp = ybuf.shape[2]

    def start_gather(tile, slot):
        base = tile * (tm * TOP_K)

        def body(r, carry):
            for kk in range(TOP_K):
                p = pos_ref[base + TOP_K * r + kk]
                pltpu.make_async_copy(yb_hbm.at[pl.ds(p, 1), :], ybuf.at[slot, pl.ds(kk * tm + r, 1), :],
                                      sem.at[slot]).start()
            return carry

        lax.fori_loop(0, tm, body, 0, unroll=8)

    def wait_gather(slot):
        pltpu.make_async_copy(yb_hbm.at[pl.ds(0, TOP_K * tm), :], ybuf.at[slot], sem.at[slot]).wait()

    slot = lax.rem(i, 2)

    @pl.when(i == 0)
    def _():
        start_gather(0, 0)

    @pl.when(i + 1 < n_tiles)
    def _():
        start_gather(i + 1, 1 - slot)

    wait_gather(slot)
    w = w_ref[...]
    a0, b0 = _unpack_pair(ybuf[slot, 0:tm, :])
    a1, b1 = _unpack_pair(ybuf[slot, tm:2 * tm, :])
    gate = mod_ref[0][5:6]
    out_a = x_ref[:, :hp] + gate[:, :hp] * (w[:, 0:1] * a0 + w[:, 1:2] * a1)
    out_b = x_ref[:, hp:] + gate[:, hp:] * (w[:, 0:1] * b0 + w[:, 1:2] * b1)
    if final_norm:
        ms = (jnp.sum(out_a * out_a, axis=-1, keepdims=True)
              + jnp.sum(out_b * out_b, axis=-1, keepdims=True)) / (2 * hp)
        inv = lax.rsqrt(ms + NORM_EPS)
        out_a = out_a * inv * fg_ref[:, :hp]
        out_b = out_b * inv * fg_ref[:, hp:]

    def store(s):
        o_refs[s][:, :hp] = out_a
        o_refs[s][:, hp:] = out_b

    _for_each_part(i, seg_tiles, store)


def _combine(x, yb, pos, weights, mod, final_g, *, final_norm, out_rows, t_ctx, l_lat):
    t, d = x.shape
    tm = _tile(min(t_ctx, l_lat), 256)
    seg_tiles = tuple(r // tm for r in out_rows)
    assert sum(seg_tiles) * tm == t
    out_specs, lo = [], 0
    for n_seg in seg_tiles:
        out_specs.append(pl.BlockSpec(
            (tm, d), lambda i, pos, lo=lo, n_seg=n_seg: (jnp.clip(i - lo, 0, n_seg - 1), 0)))
        lo += n_seg
    grid_spec = pltpu.PrefetchScalarGridSpec(
        num_scalar_prefetch=1,
        grid=(t // tm,),
        in_specs=[
            pl.BlockSpec((tm, d), lambda i, pos: (i, 0)),
            pl.BlockSpec((tm, TOP_K), lambda i, pos: (i, 0)),
            pl.BlockSpec((1, 6, d), lambda i, pos: (_mod_row(i * tm, t_ctx, l_lat), 0, 0)),
            pl.BlockSpec((1, d), lambda i, pos: (0, 0)),
            pl.BlockSpec(memory_space=pl.ANY),
        ],
        out_specs=out_specs,
        scratch_shapes=[pltpu.VMEM((2, TOP_K * tm, d // 2), I32), pltpu.SemaphoreType.DMA((2,))],
    )
    return pl.pallas_call(
        functools.partial(_combine_kernel, final_norm=final_norm, seg_tiles=seg_tiles),
        grid_spec=grid_spec,
        out_shape=[jax.ShapeDtypeStruct((r, d), F32) for r in out_rows],
        compiler_params=_params(("arbitrary",)),
        name="moe_combine",
    )(pos, x, weights, mod, final_g.reshape(1, d), yb)


def _moe(x, gamma, mod, layer, w_group, b_group, w_router, b_router, w_gate, w_up, w_down, final_g, *,
         final_norm, out_rows, t_ctx, l_lat):
    t, d = x.shape
    pad = ROUTER_LANES - N_GROUPS - N_EXPERTS
    w_all = jnp.concatenate([w_group[layer], w_router[layer], jnp.zeros((d, pad), F32)], axis=1)
    b_all = jnp.concatenate([b_group[layer], b_router[layer], jnp.zeros((pad,), F32)]).reshape(1, ROUTER_LANES)
    hp, ids, wts, cnt = _router(x, gamma, mod, w_all, b_all, t_ctx=t_ctx, l_lat=l_lat)
    n_blocks = (t * TOP_K) // MOE_ROWS + N_EXPERTS
    weights, pos, block_e, next_e, n_used, pad_lo, pad_hi = _routing_plan(ids, wts, cnt, n_blocks)
    xb = _dispatch(hp, pos, pad_lo, pad_hi, n_blocks)
    yb = _experts(xb, block_e, next_e, n_used, w_gate, w_up, w_down, layer, n_blocks)
    return _combine(x, yb, pos, weights, mod, final_g, final_norm=final_norm, out_rows=out_rows,
                    t_ctx=t_ctx, l_lat=l_lat)


def kernel(x_prompt, x_sample, c, state_ret, c_ctx, ada_w, ada_b, norm1_g, norm2_g, ret_w_qkvg, ret_w_o,
           ret_gn_g, ret_log_decay, cm_w_in, cm_ln_g, cm_ln_b, cm_w_s, cm_b_s, cm_w_out, moe_w_group,
           moe_b_group, moe_w_router, moe_b_router, moe_w_gate, moe_w_up, moe_w_down, final_norm_g):
    batch, seq, d = x_prompt.shape
    dec_batch, dec_seq, _ = x_sample.shape
    depth = ada_w.shape[0]
    n_ret = ret_w_qkvg.shape[0]
    t_ctx = batch * seq
    assert 1 + dec_batch <= COND_ROWS
    tiles = dict(t_ctx=t_ctx, l_lat=dec_seq)

    t_lat = dec_batch * dec_seq
    x_parts = [x_prompt.reshape(t_ctx, d), x_sample.reshape(t_lat, d)]
    cond = jnp.concatenate([c_ctx[None, :], c, jnp.zeros((COND_ROWS - 1 - dec_batch, d), F32)], axis=0)
    mod_all = _ada_modulation(cond, ada_w, ada_b).reshape(depth, COND_ROWS, 6, d)
    dk = d // RET_HEADS
    rope = _rope_tables(dec_seq, dk)
    w_qkvg, w_o = ret_w_qkvg.astype(BF16), ret_w_o.astype(BF16)
    w_in, w_out = cm_w_in.astype(BF16), cm_w_out.astype(BF16)

    states = None
    for layer in range(depth):
        mod = mod_all[layer]
        j = layer // 2
        last = layer == depth - 1
        if layer % 2 == 0:
            qkvg = _norm_matmul(x_parts, norm1_g[layer], mod, w_qkvg, j, shift_idx=0, scale_idx=1, act=None,
                                **tiles)
            log_g = -jnp.exp(ret_log_decay[j].astype(F32))
            o_ctx, states = _retention(qkvg, log_g, ret_gn_g[j], seq_len=seq, n_seq=batch, seq_off=0,
                                       heads_per_step=2, state_out=(j, n_ret, states))
            o_lat = _retention(qkvg, log_g, ret_gn_g[j], seq_len=dec_seq, n_seq=dec_batch,
                               seq_off=t_ctx // dec_seq, heads_per_step=1, rope_tables=rope,
                               init_state=(state_ret, j))
            a_parts = [o_ctx.reshape(t_ctx, -1), o_lat.reshape(t_lat, -1)]
            x = _proj_residual(a_parts, w_o, j, x_parts, mod, gate_idx=2, **tiles)
        else:
            uv = _norm_matmul(x_parts, norm1_g[layer], mod, w_in, j, shift_idx=0, scale_idx=1, act="gelu",
                              **tiles)
            gated = _spatial_gate(uv, cm_ln_g[j], cm_ln_b[j], cm_w_s[j], cm_b_s[j])
            x = _proj_residual([gated], w_out, j, x_parts, mod, gate_idx=2, **tiles)
        x_parts = _moe(x, norm2_g[layer], mod, layer, moe_w_group, moe_b_group, moe_w_router, moe_b_router,
                       moe_w_gate, moe_w_up, moe_w_down, final_norm_g, final_norm=last,
                       out_rows=(t_ctx, t_lat) if last else (t_ctx + t_lat,), **tiles)

    y_prompt = x_parts[0].reshape(batch, seq, d)
    y_sample = x_parts[1].reshape(dec_batch, dec_seq, d)
    return (y_prompt, y_sample, states)
```

```python
import functools

import jax
import jax.numpy as jnp
from jax import lax
from jax.experimental import pallas as pl
from jax.experimental.pallas import tpu as pltpu

F32 = jnp.float32
BF16 = jnp.bfloat16
I32 = jnp.int32

RET_HEADS = 8
GRID_W = 64
ROPE_BASE = 10000.0
CM_CHUNK = 128
CM_GROUPS = 8
N_GROUPS = 4
EXPERTS_PER_GROUP = 4
N_EXPERTS = N_GROUPS * EXPERTS_PER_GROUP
TOP_K = 2
NORM_EPS = 1e-6

MOE_ROWS = 256
ROUTER_LANES = 128
COND_ROWS = 16
RET_CHUNK_ROWS = 256
RET_UNROLL = 4
V7X_VMEM_LIMIT = 56 * 1024 * 1024
HI_MASK = -65536
NEG_LOGIT = -3.0e38


def _params(semantics):
    return pltpu.CompilerParams(dimension_semantics=semantics, vmem_limit_bytes=V7X_VMEM_LIMIT)


def _tile(n, pref):
    t = min(n, pref)
    while n % t:
        t //= 2
    return t


def _pack_pair(a, b):
    a_bits = lax.bitcast_convert_type(a.astype(BF16).astype(F32), I32)
    b_bits = lax.bitcast_convert_type(b.astype(BF16).astype(F32), I32)
    return a_bits | lax.shift_right_logical(b_bits, 16)


def _unpack_pair(w):
    a = lax.bitcast_convert_type(w & HI_MASK, F32)
    b = lax.bitcast_convert_type(lax.shift_left(w, 16), F32)
    return a, b


def _ada_kernel(cond_ref, w_ref, b_ref, o_ref):
    s = jax.nn.silu(cond_ref[...])
    o_ref[0] = jnp.dot(s.astype(BF16), w_ref[0].astype(BF16), preferred_element_type=F32) + b_ref[0]


def _ada_modulation(cond, ada_w, ada_b):
    depth, d, n = ada_w.shape
    tn = _tile(n, 1024)
    return pl.pallas_call(
        _ada_kernel,
        grid=(depth, n // tn),
        in_specs=[
            pl.BlockSpec((COND_ROWS, d), lambda l, j: (0, 0)),
            pl.BlockSpec((1, d, tn), lambda l, j: (l, 0, j)),
            pl.BlockSpec((1, 1, tn), lambda l, j: (l, 0, j)),
        ],
        out_specs=pl.BlockSpec((1, COND_ROWS, tn), lambda l, j: (l, 0, j)),
        out_shape=jax.ShapeDtypeStruct((depth, COND_ROWS, n), F32),
        compiler_params=_params(("parallel", "parallel")),
        name="ada_modulation",
    )(cond, ada_w, ada_b.reshape(depth, 1, n))


def _norm_mod(x, gamma, shift, scale):
    y = x * lax.rsqrt(jnp.mean(x * x, axis=-1, keepdims=True) + NORM_EPS)
    return (y * gamma) * (1.0 + scale) + shift


def _mod_row(tile_start, t_ctx, l_lat):
    return jnp.where(tile_start < t_ctx, 0, 1 + jnp.maximum(tile_start - t_ctx, 0) // l_lat)


def _part_specs(parts, tm, index_of, single_buffer=False):
    specs, seg_tiles, lo = [], [], 0
    extra = dict(pipeline_mode=pl.Buffered(1)) if single_buffer and len(parts) > 1 else {}
    for p in parts:
        n_tiles = p.shape[0] // tm
        specs.append(pl.BlockSpec(
            (tm, p.shape[1]),
            lambda *ids, lo=lo, n_tiles=n_tiles: (jnp.clip(index_of(*ids) - lo, 0, n_tiles - 1), 0), **extra))
        seg_tiles.append(n_tiles)
        lo += n_tiles
    return specs, tuple(seg_tiles)


def _for_each_part(i, seg_tiles, fn):
    if len(seg_tiles) == 1:
        fn(0)
        return
    lo = 0
    for s, n_tiles in enumerate(seg_tiles):
        pl.when(jnp.logical_and(i >= lo, i < lo + n_tiles))(functools.partial(fn, s))
        lo += n_tiles


def _norm_matmul_kernel(*refs, shift_idx, scale_idx, act, seg_tiles):
    n_seg = len(seg_tiles)
    x_refs = refs[:n_seg]
    g_ref, mod_ref, w_ref, o_ref, h_ref = refs[n_seg:]

    @pl.when(pl.program_id(1) == 0)
    def _():
        def prologue(s):
            m = mod_ref[0]
            h = _norm_mod(x_refs[s][...], g_ref[...], m[shift_idx:shift_idx + 1], m[scale_idx:scale_idx + 1])
            h_ref[...] = h.astype(BF16)

        _for_each_part(pl.program_id(0), seg_tiles, prologue)

    acc = jnp.dot(h_ref[...], w_ref[0], preferred_element_type=F32)
    if act == "gelu":
        acc = jax.nn.gelu(acc)
    o_ref[...] = acc.astype(o_ref.dtype)


def _norm_matmul(x_parts, gamma, mod, w_stack, w_idx, *, shift_idx, scale_idx, act, t_ctx, l_lat):
    d = x_parts[0].shape[1]
    t = sum(p.shape[0] for p in x_parts)
    n = w_stack.shape[2]
    tm = _tile(min(t_ctx, l_lat), 1024)
    tn = _tile(n, 1024)
    x_specs, seg_tiles = _part_specs(x_parts, tm, lambda i, j: i, single_buffer=True)
    return pl.pallas_call(
        functools.partial(_norm_matmul_kernel, shift_idx=shift_idx, scale_idx=scale_idx, act=act,
                          seg_tiles=seg_tiles),
        grid=(t // tm, n // tn),
        in_specs=x_specs + [
            pl.BlockSpec((1, d), lambda i, j: (0, 0)),
            pl.BlockSpec((1, 6, d), lambda i, j: (_mod_row(i * tm, t_ctx, l_lat), 0, 0)),
            pl.BlockSpec((1, d, tn), lambda i, j: (w_idx, 0, j)),
        ],
        out_specs=pl.BlockSpec((tm, tn), lambda i, j: (i, j)),
        out_shape=jax.ShapeDtypeStruct((t, n), BF16),
        scratch_shapes=[pltpu.VMEM((tm, d), BF16)],
        compiler_params=_params(("parallel", "arbitrary")),
        name="norm_matmul_" + str(act),
    )(*x_parts, gamma.reshape(1, d), mod, w_stack)


def _retention_kernel(*refs, chunk, n_chunks, hps, use_rope, has_init, emit_state, alias_state, zero_rest, dk):
    it = iter(refs)
    lg_ref, q_ref, k_ref, v_ref, g_ref, gn_ref = (next(it) for _ in range(6))
    cos_ref = sin_ref = s0_ref = st_ref = None
    if use_rope:
        cos_ref, sin_ref = next(it), next(it)
    if has_init:
        s0_ref = next(it)
    if alias_state:
        next(it)
    o_ref = next(it)
    if emit_state:
        st_ref = next(it)
    kr_ref, kf_ref, sb_ref, s_ref = (next(it) for _ in range(4))

    dv = 2 * dk
    c = chunk
    half = dk // 2
    k_scale = dk ** -0.5
    pos = lax.broadcasted_iota(jnp.int32, (c, 1), 0).astype(F32)
    rel = (lax.broadcasted_iota(jnp.int32, (c, c), 0) - lax.broadcasted_iota(jnp.int32, (c, c), 1)).astype(F32)
    tn_dims = (((0,), (0,)), ((), ()))
    nt_dims = (((1,), (1,)), ((), ()))

    def decays(hd):
        head = pl.program_id(1) * hps + hd
        lg_f = lg_ref[0, head]
        lg_b = lg_ref[1, head]
        mask = (jnp.where(rel >= 0, jnp.exp(lg_f * jnp.maximum(rel, 0.0)), 0.0)
                + jnp.where(rel <= 0, jnp.exp(lg_b * jnp.maximum(-rel, 0.0)), 0.0))
        return dict(
            qd_f=jnp.exp(lg_f * (pos + 1.0)), kd_f=jnp.exp(lg_f * (c - 1.0 - pos)) * k_scale,
            qd_b=jnp.exp(lg_b * (c - pos)), kd_b=jnp.exp(lg_b * pos) * k_scale,
            cd_f=jnp.exp(jnp.full((1, 1), c, F32) * lg_f), cd_b=jnp.exp(jnp.full((1, 1), c, F32) * lg_b),
            mask=mask * k_scale)

    dec = [decays(hd) for hd in range(hps)]

    def rows(ci):
        if isinstance(ci, int):
            return pl.ds(ci * c, c)
        return pl.ds(pl.multiple_of(ci * c, c), c)

    def kcols(hd):
        return slice(hd * dk, (hd + 1) * dk)

    def vcols(hd):
        return slice(hd * dv, (hd + 1) * dv)

    def rotated(ref, sl, hd):
        x = ref[0, sl, kcols(hd)].astype(F32)
        if use_rope:
            cos = cos_ref[sl, :]
            sin = sin_ref[sl, :]
            x1, x2 = x[:, :half], x[:, half:]
            x = jnp.concatenate([x1 * cos - x2 * sin, x1 * sin + x2 * cos], axis=-1)
        return x

    def emit(direction, hd, value):
        st_ref[0, 0, direction, hd] = value
        if zero_rest:
            for other in range(1, st_ref.shape[1]):
                st_ref[0, other, direction, hd] = jnp.zeros_like(value)

    def finish(sl, hd, o):
        mu = jnp.mean(o, axis=-1, keepdims=True)
        dlt = o - mu
        var = jnp.mean(dlt * dlt, axis=-1, keepdims=True)
        on = dlt * lax.rsqrt(var + NORM_EPS)
        gate = g_ref[0, sl, vcols(hd)].astype(F32)
        o_ref[0, sl, vcols(hd)] = ((on * gn_ref[hd]) * jax.nn.silu(gate)).astype(o_ref.dtype)

    def scores(q, kb, hd):
        a = lax.dot_general(q.astype(BF16), kb, nt_dims, preferred_element_type=F32) * dec[hd]["mask"]
        return a.astype(BF16)

    if n_chunks == 1 and not has_init:
        sl = rows(0)
        for hd in range(hps):
            q = rotated(q_ref, sl, hd)
            k = rotated(k_ref, sl, hd)
            v = v_ref[0, sl, vcols(hd)]
            finish(sl, hd, jnp.dot(scores(q, k.astype(BF16), hd), v, preferred_element_type=F32))
            if emit_state:
                for direction, kd in ((0, dec[hd]["kd_f"]), (1, dec[hd]["kd_b"])):
                    emit(direction, hd,
                         lax.dot_general((k * kd).astype(BF16), v, tn_dims, preferred_element_type=F32))
        return

    def init_states(direction):
        for hd in range(hps):
            if has_init:
                s_ref[hd] = s0_ref[0, 0, direction, hd]
            else:
                s_ref[hd] = jnp.zeros(s_ref.shape[1:], F32)

    init_states(1)

    def bwd_body(t, carry):
        ci = n_chunks - 1 - t
        sl = rows(ci)
        for hd in range(hps):
            k = rotated(k_ref, sl, hd)
            kr_ref[sl, kcols(hd)] = k.astype(BF16)
            kf_ref[sl, kcols(hd)] = (k * dec[hd]["kd_f"]).astype(BF16)
            sb_ref[ci * hps + hd] = s_ref[hd].astype(BF16)
            kv = lax.dot_general((k * dec[hd]["kd_b"]).astype(BF16), v_ref[0, sl, vcols(hd)], tn_dims,
                                 preferred_element_type=F32)
            s_ref[hd] = dec[hd]["cd_b"] * s_ref[hd] + kv
        return carry

    lax.fori_loop(0, n_chunks, bwd_body, 0, unroll=RET_UNROLL)
    if emit_state:
        for hd in range(hps):
            emit(1, hd, s_ref[hd])

    init_states(0)

    def fwd_body(ci, carry):
        sl = rows(ci)
        for hd in range(hps):
            q = rotated(q_ref, sl, hd)
            v = v_ref[0, sl, vcols(hd)]
            o = (jnp.dot(scores(q, kr_ref[sl, kcols(hd)], hd), v, preferred_element_type=F32)
                 + jnp.dot((q * dec[hd]["qd_f"]).astype(BF16), s_ref[hd].astype(BF16),
                           preferred_element_type=F32)
                 + jnp.dot((q * dec[hd]["qd_b"]).astype(BF16), sb_ref[ci * hps + hd],
                           preferred_element_type=F32))
            finish(sl, hd, o)
            kv = lax.dot_general(kf_ref[sl, kcols(hd)], v, tn_dims, preferred_element_type=F32)
            s_ref[hd] = dec[hd]["cd_f"] * s_ref[hd] + kv
        return carry

    lax.fori_loop(0, n_chunks, fwd_body, 0, unroll=RET_UNROLL)
    if emit_state:
        for hd in range(hps):
            emit(0, hd, s_ref[hd])


def _retention(qkvg, log_g, gn_g, *, seq_len, n_seq, seq_off, heads_per_step, rope_tables=None,
               init_state=None, state_out=None):
    t, width = qkvg.shape
    h = RET_HEADS
    hps = heads_per_step
    dk = width // (6 * h)
    dv = 2 * dk
    chunk = min(RET_CHUNK_ROWS, seq_len)
    n_chunks = seq_len // chunk
    x3 = qkvg.reshape(t // seq_len, seq_len, width)
    k_blk, v_blk, g_blk = h // hps, (2 * h * dk) // (hps * dv), (2 * h * dk + h * dv) // (hps * dv)
    use_rope = rope_tables is not None
    has_init = init_state is not None
    emit_state = state_out is not None

    in_specs = [
        pl.BlockSpec(memory_space=pltpu.SMEM),
        pl.BlockSpec((1, seq_len, hps * dk), lambda b, hh: (b + seq_off, 0, hh)),
        pl.BlockSpec((1, seq_len, hps * dk), lambda b, hh: (b + seq_off, 0, k_blk + hh)),
        pl.BlockSpec((1, seq_len, hps * dv), lambda b, hh: (b + seq_off, 0, v_blk + hh)),
        pl.BlockSpec((1, seq_len, hps * dv), lambda b, hh: (b + seq_off, 0, g_blk + hh)),
        pl.BlockSpec((hps, 1, dv), lambda b, hh: (hh, 0, 0)),
    ]
    args = [log_g, x3, x3, x3, x3, gn_g.reshape(h, 1, dv)]
    if use_rope:
        in_specs += [pl.BlockSpec((seq_len, dk // 2), lambda b, hh: (0, 0))] * 2
        args += list(rope_tables)
    if has_init:
        layer = init_state[1]
        in_specs.append(pl.BlockSpec((1, 1, 2, hps, dk, dv), lambda b, hh: (b, layer, 0, hh, 0, 0)))
        args.append(init_state[0])
    out_specs = [pl.BlockSpec((1, seq_len, hps * dv), lambda b, hh: (b, 0, hh))]
    out_shape = [jax.ShapeDtypeStruct((n_seq, seq_len, h * dv), BF16)]
    aliases = {}
    alias_state = zero_rest = False
    if emit_state:
        s_layer, n_layers, prev = state_out
        alias_state = prev is not None
        zero_rest = not alias_state and n_layers > 1
        if alias_state:
            aliases = {len(args): 1}
            in_specs.append(pl.BlockSpec(memory_space=pl.ANY))
            args.append(prev)
            out_specs.append(pl.BlockSpec((1, 1, 2, hps, dk, dv), lambda b, hh: (b, s_layer, 0, hh, 0, 0)))
        else:
            assert s_layer == 0
            out_specs.append(pl.BlockSpec((1, n_layers, 2, hps, dk, dv), lambda b, hh: (b, 0, 0, hh, 0, 0)))
        out_shape.append(jax.ShapeDtypeStruct((n_seq, n_layers, 2, h, dk, dv), F32))
    outs = pl.pallas_call(
        functools.partial(_retention_kernel, chunk=chunk, n_chunks=n_chunks, hps=hps, use_rope=use_rope,
                          has_init=has_init, emit_state=emit_state, alias_state=alias_state,
                          zero_rest=zero_rest, dk=dk),
        grid=(n_seq, h // hps),
        in_specs=in_specs,
        out_specs=out_specs,
        out_shape=out_shape,
        scratch_shapes=[pltpu.VMEM((seq_len, hps * dk), BF16), pltpu.VMEM((seq_len, hps * dk), BF16),
                        pltpu.VMEM((n_chunks * hps, dk, dv), BF16), pltpu.VMEM((hps, dk, dv), F32)],
        input_output_aliases=aliases,
        compiler_params=_params(("parallel", "parallel")),
        name="retention_rope" if use_rope else "retention_ctx",
    )(*args)
    return outs if emit_state else outs[0]


def _rope_tables(seq_len, dk):
    rows = seq_len // GRID_W
    row = jnp.broadcast_to(jnp.arange(rows)[:, None], (rows, GRID_W)).reshape(-1).astype(F32)
    col = jnp.broadcast_to(jnp.arange(GRID_W)[None, :], (rows, GRID_W)).reshape(-1).astype(F32)
    n_freq = dk // 4
    inv = ROPE_BASE ** (-jnp.arange(n_freq, dtype=F32) / n_freq)
    ang = jnp.concatenate([row[:, None] * inv, col[:, None] * inv], axis=-1)
    return jnp.cos(ang), jnp.sin(ang)


def _proj_residual_kernel(*refs, gate_idx, seg_tiles, n_x):
    n_seg = len(seg_tiles)
    a_refs = refs[:n_seg]
    x_refs = refs[n_seg:n_seg + n_x]
    w_ref, mod_ref, o_ref = refs[n_seg + n_x:]
    gate = mod_ref[0][gate_idx:gate_idx + 1]

    def seg(s):
        x_ref = x_refs[s] if n_x > 1 else x_refs[0]
        acc = jnp.dot(a_refs[s][...], w_ref[0], preferred_element_type=F32)
        o_ref[...] = x_ref[...] + gate * acc

    _for_each_part(pl.program_id(1), seg_tiles, seg)


def _proj_residual(a_parts, w_stack, w_idx, x_parts, mod, *, gate_idx, t_ctx, l_lat):
    d = x_parts[0].shape[1]
    t = sum(p.shape[0] for p in a_parts)
    k = w_stack.shape[1]
    tm = _tile(min(t_ctx, l_lat), 512)
    tn = _tile(d, 1024)
    a_specs, seg_tiles = _part_specs(a_parts, tm, lambda j, i: i)
    assert sum(seg_tiles) * tm == t
    x_specs = []
    lo = 0
    for p in x_parts:
        n_tiles = p.shape[0] // tm
        x_specs.append(pl.BlockSpec(
            (tm, tn), lambda j, i, lo=lo, n_tiles=n_tiles: (jnp.clip(i - lo, 0, n_tiles - 1), j)))
        lo += n_tiles
    assert len(x_parts) == 1 or [p.shape[0] for p in x_parts] == [p.shape[0] for p in a_parts]
    return pl.pallas_call(
        functools.partial(_proj_residual_kernel, gate_idx=gate_idx, seg_tiles=seg_tiles, n_x=len(x_parts)),
        grid=(d // tn, t // tm),
        in_specs=a_specs + x_specs + [
            pl.BlockSpec((1, k, tn), lambda j, i: (w_idx, 0, j)),
            pl.BlockSpec((1, 6, tn), lambda j, i: (_mod_row(i * tm, t_ctx, l_lat), 0, j)),
        ],
        out_specs=pl.BlockSpec((tm, tn), lambda j, i: (i, j)),
        out_shape=jax.ShapeDtypeStruct((t, d), F32),
        compiler_params=_params(("parallel", "parallel")),
        name="proj_residual",
    )(*a_parts, *x_parts, w_stack, mod)


def _spatial_gate_kernel(u_ref, v_ref, lng_ref, lnb_ref, ws_ref, bs_ref, t_ref, vn_ref, *, n_sub, cw):
    v32 = v_ref[...].astype(F32)
    mu = jnp.mean(v32, axis=-1, keepdims=True)
    dlt = v32 - mu
    var = jnp.mean(dlt * dlt, axis=-1, keepdims=True)
    vn_ref[...] = (dlt * lax.rsqrt(var + NORM_EPS) * lng_ref[...] + lnb_ref[...]).astype(BF16)
    for n in range(n_sub):
        rs = slice(n * CM_CHUNK, (n + 1) * CM_CHUNK)
        for g in range(CM_GROUPS):
            cs = slice(g * cw, (g + 1) * cw)
            s = jnp.dot(ws_ref[g], vn_ref[rs, cs], preferred_element_type=F32) + bs_ref[g]
            t_ref[rs, cs] = (u_ref[rs, cs].astype(F32) * s).astype(t_ref.dtype)


def _spatial_gate(uv, ln_g, ln_b, w_s, b_s):
    t, two_cw = uv.shape
    width = two_cw // 2
    cw = width // CM_GROUPS
    tm = 2 * CM_CHUNK
    return pl.pallas_call(
        functools.partial(_spatial_gate_kernel, n_sub=tm // CM_CHUNK, cw=cw),
        grid=(t // tm,),
        in_specs=[
            pl.BlockSpec((tm, width), lambda i: (i, 0)),
            pl.BlockSpec((tm, width), lambda i: (i, 1)),
            pl.BlockSpec((1, width), lambda i: (0, 0)),
            pl.BlockSpec((1, width), lambda i: (0, 0)),
            pl.BlockSpec((CM_GROUPS, CM_CHUNK, CM_CHUNK), lambda i: (0, 0, 0)),
            pl.BlockSpec((CM_GROUPS, CM_CHUNK, 1), lambda i: (0, 0, 0)),
        ],
        out_specs=pl.BlockSpec((tm, width), lambda i: (i, 0)),
        out_shape=jax.ShapeDtypeStruct((t, width), BF16),
        scratch_shapes=[pltpu.VMEM((tm, width), BF16)],
        compiler_params=_params(("parallel",)),
        name="spatial_gate",
    )(uv, uv, ln_g.reshape(1, width), ln_b.reshape(1, width), w_s.astype(BF16),
      b_s.reshape(CM_GROUPS, CM_CHUNK, 1))


def _router_kernel(x_ref, g_ref, mod_ref, wr_ref, br_ref, h_ref, ids_ref, wts_ref, cnt_ref, run_ref):
    m = mod_ref[0]
    h = _norm_mod(x_ref[...], g_ref[...], m[3:4], m[4:5])
    hp = h.shape[1] // 2
    h_ref[...] = _pack_pair(h[:, :hp], h[:, hp:])
    h_hi = h.astype(BF16)
    h_lo = (h - h_hi.astype(F32)).astype(BF16)
    w = wr_ref[...]
    w_hi = w.astype(BF16)
    w_lo = (w - w_hi.astype(F32)).astype(BF16)
    lg = (jnp.dot(h_hi, w_hi, preferred_element_type=F32)
          + (jnp.dot(h_lo, w_hi, preferred_element_type=F32) + jnp.dot(h_hi, w_lo, preferred_element_type=F32)))
    lg = lg + br_ref[...]

    tm = lg.shape[0]
    lane = lax.broadcasted_iota(I32, lg.shape, 1).astype(F32)
    far = float(ROUTER_LANES)

    def first_max(vals):
        top = jnp.max(vals, axis=-1, keepdims=True)
        return top, jnp.min(jnp.where(vals == top, lane, far), axis=-1, keepdims=True)

    in_groups = lane < N_GROUPS
    g_top, grp = first_max(jnp.where(in_groups, lg, NEG_LOGIT))
    p_grp = 1.0 / jnp.sum(jnp.where(in_groups, jnp.exp(lg - g_top), 0.0), axis=-1, keepdims=True)
    lo = N_GROUPS + EXPERTS_PER_GROUP * grp
    cand = jnp.where(jnp.logical_and(lane >= lo, lane < lo + EXPERTS_PER_GROUP), lg, NEG_LOGIT)
    v1, i1 = first_max(cand)
    v2, i2 = first_max(jnp.where(lane == i1, NEG_LOGIT, cand))
    e2 = jnp.exp(v2 - v1)
    w1 = (1.0 / (1.0 + e2)) * p_grp
    w2 = (e2 / (1.0 + e2)) * p_grp

    @pl.when(pl.program_id(0) == 0)
    def _():
        run_ref[...] = jnp.zeros_like(run_ref)

    oh1 = (lane == i1).astype(F32)
    oh2 = (lane == i2).astype(F32)
    both = oh1 + oh2
    earlier = (lax.broadcasted_iota(I32, (tm, tm), 0) > lax.broadcasted_iota(I32, (tm, tm), 1))
    before = jnp.dot(earlier.astype(BF16), both.astype(BF16), preferred_element_type=F32) + run_ref[0:1, :]
    r1 = jnp.sum(oh1 * before, axis=-1, keepdims=True)
    r2 = jnp.sum(oh2 * before, axis=-1, keepdims=True)
    run_ref[...] = run_ref[...] + jnp.sum(both, axis=0, keepdims=True)
    cnt_ref[...] = run_ref[...].astype(I32)

    ids = jnp.where(lane == 0, i1 - N_GROUPS,
                    jnp.where(lane == 1, i2 - N_GROUPS, jnp.where(lane == 2, r1, jnp.where(lane == 3, r2, 0.0))))
    ids_ref[...] = ids.astype(I32)
    wts_ref[...] = jnp.where(lane == 0, w1, jnp.where(lane == 1, w2, 0.0))


def _router(x, gamma, mod, w_router_all, b_router_all, *, t_ctx, l_lat):
    t, d = x.shape
    tm = _tile(min(t_ctx, l_lat), 512)
    return pl.pallas_call(
        _router_kernel,
        grid=(t // tm,),
        in_specs=[
            pl.BlockSpec((tm, d), lambda i: (i, 0)),
            pl.BlockSpec((1, d), lambda i: (0, 0)),
            pl.BlockSpec((1, 6, d), lambda i: (_mod_row(i * tm, t_ctx, l_lat), 0, 0)),
            pl.BlockSpec((d, ROUTER_LANES), lambda i: (0, 0)),
            pl.BlockSpec((1, ROUTER_LANES), lambda i: (0, 0)),
        ],
        out_specs=[
            pl.BlockSpec((tm, d // 2), lambda i: (i, 0)),
            pl.BlockSpec((tm, ROUTER_LANES), lambda i: (i, 0)),
            pl.BlockSpec((tm, ROUTER_LANES), lambda i: (i, 0)),
            pl.BlockSpec((8, ROUTER_LANES), lambda i: (0, 0)),
        ],
        out_shape=[jax.ShapeDtypeStruct((t, d // 2), I32), jax.ShapeDtypeStruct((t, ROUTER_LANES), I32),
                   jax.ShapeDtypeStruct((t, ROUTER_LANES), F32), jax.ShapeDtypeStruct((8, ROUTER_LANES), I32)],
        scratch_shapes=[pltpu.VMEM((8, ROUTER_LANES), F32)],
        compiler_params=_params(("arbitrary",)),
        name="moe_router",
    )(x, gamma.reshape(1, d), mod, w_router_all, b_router_all)


def _routing_plan(ids, wts, cnt, n_blocks):
    expert_ids = ids[:, 0:TOP_K].reshape(-1)
    rank = ids[:, TOP_K:2 * TOP_K].reshape(-1)
    weights = wts[:, 0:TOP_K]
    counts = cnt[0, N_GROUPS:N_GROUPS + N_EXPERTS]
    e_range = jnp.arange(N_EXPERTS, dtype=I32)
    onehot = (expert_ids[:, None] == e_range[None, :]).astype(I32)
    padded = (counts + MOE_ROWS - 1) // MOE_ROWS * MOE_ROWS
    pad_ends = jnp.cumsum(padded)
    pad_starts = pad_ends - padded
    pos = (jnp.sum(onehot * pad_starts[None, :], axis=1) + rank).astype(I32)
    blk_start = jnp.arange(n_blocks, dtype=I32) * MOE_ROWS
    block_e = jnp.minimum(jnp.sum((pad_ends[None, :] <= blk_start[:, None]).astype(I32), axis=1),
                          N_EXPERTS - 1).astype(I32)
    n_used = (pad_ends[-1] // MOE_ROWS).astype(I32).reshape(1)
    later = jnp.where((counts[None, :] > 0) & (e_range[None, :] > e_range[:, None]), e_range[None, :], N_EXPERTS)
    next_of = jnp.min(later, axis=1)
    next_of = jnp.where(next_of == N_EXPERTS, -1, next_of).astype(I32)
    next_e = jnp.sum((block_e[:, None] == e_range[None, :]).astype(I32) * next_of[None, :], axis=1).astype(I32)
    pad_lo = (pad_starts + counts).astype(I32)
    return weights, pos, block_e, next_e, n_used, pad_lo, pad_ends.astype(I32)


def _dispatch_kernel(pos_ref, lo_ref, hi_ref, h_ref, xb_hbm, zero_ref, sem, zsem):
    i = pl.program_id(0)
    tm = h_ref.shape[0]
    base = i * (tm * TOP_K)

    for r in range(tm):
        for kk in range(TOP_K):
            p = pos_ref[base + (TOP_K * r + kk)]
            pltpu.make_async_copy(h_ref.at[pl.ds(r, 1), :], xb_hbm.at[pl.ds(p, 1), :], sem).start()
    for _ in range(TOP_K):
        pltpu.make_async_copy(h_ref, xb_hbm.at[pl.ds(0, tm), :], sem).wait()

    @pl.when(i == pl.num_programs(0) - 1)
    def _():
        zero_ref[...] = jnp.zeros_like(zero_ref)

        def zero_row(p):
            return pltpu.make_async_copy(zero_ref.at[pl.ds(0, 1), :], xb_hbm.at[pl.ds(p, 1), :], zsem)

        def per_expert(e, carry):
            lo = lo_ref[e]
            hi = hi_ref[e]

            def start(p, c):
                zero_row(p).start()
                return c

            def wait(p, c):
                zero_row(p).wait()
                return c

            lax.fori_loop(lo, hi, start, 0)
            lax.fori_loop(lo, hi, wait, 0)
            return carry

        lax.fori_loop(0, N_EXPERTS, per_expert, 0)

        n_blocks = xb_hbm.shape[0] // MOE_ROWS
        n_used = hi_ref[N_EXPERTS - 1] // MOE_ROWS

        def zero_block(blk):
            dst = xb_hbm.at[pl.ds(pl.multiple_of(blk * MOE_ROWS, MOE_ROWS), MOE_ROWS), :]
            return pltpu.make_async_copy(zero_ref, dst, zsem)

        def start_block(blk, c):
            zero_block(blk).start()
            return c

        def wait_block(blk, c):
            zero_block(blk).wait()
            return c

        lax.fori_loop(n_used, n_blocks, start_block, 0)
        lax.fori_loop(n_used, n_blocks, wait_block, 0)


def _dispatch(hp, pos, pad_lo, pad_hi, n_blocks):
    t, width = hp.shape
    tm = _tile(t, 512)
    grid_spec = pltpu.PrefetchScalarGridSpec(
        num_scalar_prefetch=3,
        grid=(t // tm,),
        in_specs=[pl.BlockSpec((tm, width), lambda i, pos, lo, hi: (i, 0))],
        out_specs=pl.BlockSpec(memory_space=pl.ANY),
        scratch_shapes=[pltpu.VMEM((MOE_ROWS, width), I32), pltpu.SemaphoreType.DMA(()),
                        pltpu.SemaphoreType.DMA(())],
    )
    return pl.pallas_call(
        _dispatch_kernel,
        grid_spec=grid_spec,
        out_shape=jax.ShapeDtypeStruct((n_blocks * MOE_ROWS, width), I32),
        compiler_params=_params(("arbitrary",)),
        name="moe_dispatch",
    )(pos, pad_lo, pad_hi, hp)


def _expert_kernel(be_ref, nx_ref, nu_ref, x_ref, wg_hbm, wu_hbm, wd_hbm, y_ref,
                   sg_ref, su_ref, sd_ref, wg_ref, wu_ref, wd_ref, sem, *, layer):
    i = pl.program_id(0)
    n_used = nu_ref[0]
    e = be_ref[i]
    changed = jnp.logical_or(i == 0, e != be_ref[jnp.maximum(i - 1, 0)])
    hp = x_ref.shape[1]

    def weight_copies(ex):
        return (pltpu.make_async_copy(wg_hbm.at[layer, ex], sg_ref, sem.at[0]),
                pltpu.make_async_copy(wu_hbm.at[layer, ex], su_ref, sem.at[1]),
                pltpu.make_async_copy(wd_hbm.at[layer, ex], sd_ref, sem.at[2]))

    @pl.when(i == 0)
    def _():
        for cp in weight_copies(e):
            cp.start()

    @pl.when(jnp.logical_and(changed, i < n_used))
    def _():
        for cp in weight_copies(e):
            cp.wait()
        for src, dst in ((sg_ref, wg_ref), (su_ref, wu_ref), (sd_ref, wd_ref)):
            rows = src.shape[0]
            step = 256

            def cast(ci, carry, src=src, dst=dst):
                sl = pl.ds(pl.multiple_of(ci * step, step), step)
                dst[sl, :] = src[sl, :].astype(BF16)
                return carry

            lax.fori_loop(0, rows // step, cast, 0)
        nxt = nx_ref[i]

        @pl.when(nxt >= 0)
        def _():
            for cp in weight_copies(nxt):
                cp.start()

    @pl.when(i < n_used)
    def _():
        xa, xb = _unpack_pair(x_ref[...])
        x = jnp.concatenate([xa.astype(BF16), xb.astype(BF16)], axis=1)
        hg = jnp.dot(x, wg_ref[...], preferred_element_type=F32)
        hu = jnp.dot(x, wu_ref[...], preferred_element_type=F32)
        hm = (jax.nn.silu(hg) * hu).astype(BF16)
        y = jnp.dot(hm, wd_ref[...], preferred_element_type=F32)
        y_ref[...] = _pack_pair(y[:, :hp], y[:, hp:])

    @pl.when(i >= n_used)
    def _():
        y_ref[...] = jnp.zeros_like(y_ref)


def _experts(xb, block_e, next_e, n_used, w_gate, w_up, w_down, layer, n_blocks):
    width = xb.shape[1]
    _, _, d, hid = w_gate.shape
    grid_spec = pltpu.PrefetchScalarGridSpec(
        num_scalar_prefetch=3,
        grid=(n_blocks,),
        in_specs=[
            pl.BlockSpec((MOE_ROWS, width), lambda i, be, nx, nu: (jnp.minimum(i, nu[0] - 1), 0)),
            pl.BlockSpec(memory_space=pl.ANY),
            pl.BlockSpec(memory_space=pl.ANY),
            pl.BlockSpec(memory_space=pl.ANY),
        ],
        out_specs=pl.BlockSpec((MOE_ROWS, width), lambda i, be, nx, nu: (i, 0)),
        scratch_shapes=[
            pltpu.VMEM((d, hid), F32), pltpu.VMEM((d, hid), F32), pltpu.VMEM((hid, d), F32),
            pltpu.VMEM((d, hid), BF16), pltpu.VMEM((d, hid), BF16), pltpu.VMEM((hid, d), BF16),
            pltpu.SemaphoreType.DMA((3,)),
        ],
    )
    return pl.pallas_call(
        functools.partial(_expert_kernel, layer=layer),
        grid_spec=grid_spec,
        out_shape=jax.ShapeDtypeStruct((n_blocks * MOE_ROWS, width), I32),
        compiler_params=_params(("arbitrary",)),
        name="moe_experts",
    )(block_e, next_e, n_used, xb, w_gate, w_up, w_down)


def _combine_kernel(pos_ref, x_ref, w_ref, mod_ref, fg_ref, yb_hbm, *rest, final_norm, seg_tiles):
    o_refs = rest[:len(seg_tiles)]
    ybuf, sem = rest[len(seg_tiles):]
    i = pl.program_id(0)
    n_tiles = pl.num_programs(0)
    tm = x_ref.shape[0]
    hp = ybuf.shape[2]

    def start_gather(tile, slot):
        base = tile * (tm * TOP_K)

        for r in range(tm):
            for kk in range(TOP_K):
                p = pos_ref[base + (TOP_K * r + kk)]
                pltpu.make_async_copy(yb_hbm.at[pl.ds(p, 1), :], ybuf.at[slot, pl.ds(kk * tm + r, 1), :],
                                      sem.at[slot]).start()

    def wait_gather(slot):
        pltpu.make_async_copy(yb_hbm.at[pl.ds(0, TOP_K * tm), :], ybuf.at[slot], sem.at[slot]).wait()

    slot = lax.rem(i, 2)

    @pl.when(i == 0)
    def _():
        start_gather(0, 0)

    @pl.when(i + 1 < n_tiles)
    def _():
        start_gather(i + 1, 1 - slot)

    wait_gather(slot)
    w = w_ref[...]
    a0, b0 = _unpack_pair(ybuf[slot, 0:tm, :])
    a1, b1 = _unpack_pair(ybuf[slot, tm:2 * tm, :])
    gate = mod_ref[0][5:6]
    out_a = x_ref[:, :hp] + gate[:, :hp] * (w[:, 0:1] * a0 + w[:, 1:2] * a1)
    out_b = x_ref[:, hp:] + gate[:, hp:] * (w[:, 0:1] * b0 + w[:, 1:2] * b1)
    if final_norm:
        ms = (jnp.sum(out_a * out_a, axis=-1, keepdims=True)
              + jnp.sum(out_b * out_b, axis=-1, keepdims=True)) / (2 * hp)
        inv = lax.rsqrt(ms + NORM_EPS)
        out_a = out_a * inv * fg_ref[:, :hp]
        out_b = out_b * inv * fg_ref[:, hp:]

    def store(s):
        o_refs[s][:, :hp] = out_a
        o_refs[s][:, hp:] = out_b

    _for_each_part(i, seg_tiles, store)


def _combine(x, yb, pos, weights, mod, final_g, *, final_norm, out_rows, t_ctx, l_lat):
    t, d = x.shape
    tm = _tile(min(t_ctx, l_lat), 256)
    seg_tiles = tuple(r // tm for r in out_rows)
    assert sum(seg_tiles) * tm == t
    out_specs, lo = [], 0
    for n_seg in seg_tiles:
        out_specs.append(pl.BlockSpec(
            (tm, d), lambda i, pos, lo=lo, n_seg=n_seg: (jnp.clip(i - lo, 0, n_seg - 1), 0)))
        lo += n_seg
    grid_spec = pltpu.PrefetchScalarGridSpec(
        num_scalar_prefetch=1,
        grid=(t // tm,),
        in_specs=[
            pl.BlockSpec((tm, d), lambda i, pos: (i, 0)),
            pl.BlockSpec((tm, TOP_K), lambda i, pos: (i, 0)),
            pl.BlockSpec((1, 6, d), lambda i, pos: (_mod_row(i * tm, t_ctx, l_lat), 0, 0)),
            pl.BlockSpec((1, d), lambda i, pos: (0, 0)),
            pl.BlockSpec(memory_space=pl.ANY),
        ],
        out_specs=out_specs,
        scratch_shapes=[pltpu.VMEM((2, TOP_K * tm, d // 2), I32), pltpu.SemaphoreType.DMA((2,))],
    )
    return pl.pallas_call(
        functools.partial(_combine_kernel, final_norm=final_norm, seg_tiles=seg_tiles),
        grid_spec=grid_spec,
        out_shape=[jax.ShapeDtypeStruct((r, d), F32) for r in out_rows],
        compiler_params=_params(("arbitrary",)),
        name="moe_combine",
    )(pos, x, weights, mod, final_g.reshape(1, d), yb)


def _moe(x, gamma, mod, layer, w_group, b_group, w_router, b_router, w_gate, w_up, w_down, final_g, *,
         final_norm, out_rows, t_ctx, l_lat):
    t, d = x.shape
    pad = ROUTER_LANES - N_GROUPS - N_EXPERTS
    w_all = jnp.concatenate([w_group[layer], w_router[layer], jnp.zeros((d, pad), F32)], axis=1)
    b_all = jnp.concatenate([b_group[layer], b_router[layer], jnp.zeros((pad,), F32)]).reshape(1, ROUTER_LANES)
    hp, ids, wts, cnt = _router(x, gamma, mod, w_all, b_all, t_ctx=t_ctx, l_lat=l_lat)
    n_blocks = (t * TOP_K) // MOE_ROWS + N_EXPERTS
    weights, pos, block_e, next_e, n_used, pad_lo, pad_hi = _routing_plan(ids, wts, cnt, n_blocks)
    xb = _dispatch(hp, pos, pad_lo, pad_hi, n_blocks)
    yb = _experts(xb, block_e, next_e, n_used, w_gate, w_up, w_down, layer, n_blocks)
    return _combine(x, yb, pos, weights, mod, final_g, final_norm=final_norm, out_rows=out_rows,
                    t_ctx=t_ctx, l_lat=l_lat)


def kernel(x_prompt, x_sample, c, state_ret, c_ctx, ada_w, ada_b, norm1_g, norm2_g, ret_w_qkvg, ret_w_o,
           ret_gn_g, ret_log_decay, cm_w_in, cm_ln_g, cm_ln_b, cm_w_s, cm_b_s, cm_w_out, moe_w_group,
           moe_b_group, moe_w_router, moe_b_router, moe_w_gate, moe_w_up, moe_w_down, final_norm_g):
    batch, seq, d = x_prompt.shape
    dec_batch, dec_seq, _ = x_sample.shape
    depth = ada_w.shape[0]
    n_ret = ret_w_qkvg.shape[0]
    t_ctx = batch * seq
    assert 1 + dec_batch <= COND_ROWS
    tiles = dict(t_ctx=t_ctx, l_lat=dec_seq)

    t_lat = dec_batch * dec_seq
    x_parts = [x_prompt.reshape(t_ctx, d), x_sample.reshape(t_lat, d)]
    cond = jnp.concatenate([c_ctx[None, :], c, jnp.zeros((COND_ROWS - 1 - dec_batch, d), F32)], axis=0)
    mod_all = _ada_modulation(cond, ada_w, ada_b).reshape(depth, COND_ROWS, 6, d)
    dk = d // RET_HEADS
    rope = _rope_tables(dec_seq, dk)
    w_qkvg, w_o = ret_w_qkvg.astype(BF16), ret_w_o.astype(BF16)
    w_in, w_out = cm_w_in.astype(BF16), cm_w_out.astype(BF16)

    states = None
    for layer in range(depth):
        mod = mod_all[layer]
        j = layer // 2
        last = layer == depth - 1
        if layer % 2 == 0:
            qkvg = _norm_matmul(x_parts, norm1_g[layer], mod, w_qkvg, j, shift_idx=0, scale_idx=1, act=None,
                                **tiles)
            log_g = -jnp.exp(ret_log_decay[j].astype(F32))
            o_ctx, states = _retention(qkvg, log_g, ret_gn_g[j], seq_len=seq, n_seq=batch, seq_off=0,
                                       heads_per_step=2, state_out=(j, n_ret, states))
            o_lat = _retention(qkvg, log_g, ret_gn_g[j], seq_len=dec_seq, n_seq=dec_batch,
                               seq_off=t_ctx // dec_seq, heads_per_step=1, rope_tables=rope,
                               init_state=(state_ret, j))
            a_parts = [o_ctx.reshape(t_ctx, -1), o_lat.reshape(t_lat, -1)]
            x = _proj_residual(a_parts, w_o, j, x_parts, mod, gate_idx=2, **tiles)
        else:
            uv = _norm_matmul(x_parts, norm1_g[layer], mod, w_in, j, shift_idx=0, scale_idx=1, act="gelu",
                              **tiles)
            gated = _spatial_gate(uv, cm_ln_g[j], cm_ln_b[j], cm_w_s[j], cm_b_s[j])
            x = _proj_residual([gated], w_out, j, x_parts, mod, gate_idx=2, **tiles)
        x_parts = _moe(x, norm2_g[layer], mod, layer, moe_w_group, moe_b_group, moe_w_router, moe_b_router,
                       moe_w_gate, moe_w_up, moe_w_down, final_norm_g, final_norm=last,
                       out_rows=(t_ctx, t_lat) if last else (t_ctx + t_lat,), **tiles)

    y_prompt = x_parts[0].reshape(batch, seq, d)
    y_sample = x_parts[1].reshape(dec_batch, dec_seq, d)
    return (y_prompt, y_sample, states)
```

```python
import functools

import jax
import jax.numpy as jnp
from jax import lax
from jax.experimental import pallas as pl
from jax.experimental.pallas import tpu as pltpu

F32 = jnp.float32
BF16 = jnp.bfloat16
I32 = jnp.int32

RET_HEADS = 8
GRID_W = 64
ROPE_BASE = 10000.0
CM_CHUNK = 128
CM_GROUPS = 8
N_GROUPS = 4
EXPERTS_PER_GROUP = 4
N_EXPERTS = N_GROUPS * EXPERTS_PER_GROUP
TOP_K = 2
NORM_EPS = 1e-6

MOE_ROWS = 256
ROUTER_LANES = 128
COND_ROWS = 16
RET_CHUNK_ROWS = 256
RET_UNROLL = 8
V7X_VMEM_LIMIT = 56 * 1024 * 1024
HI_MASK = -65536
NEG_LOGIT = -3.0e38


def _params(semantics):
    return pltpu.CompilerParams(dimension_semantics=semantics, vmem_limit_bytes=V7X_VMEM_LIMIT)


def _tile(n, pref):
    t = min(n, pref)
    while n % t:
        t //= 2
    return t


def _pack_pair(a, b):
    a_bits = lax.bitcast_convert_type(a.astype(BF16).astype(F32), I32)
    b_bits = lax.bitcast_convert_type(b.astype(BF16).astype(F32), I32)
    return a_bits | lax.shift_right_logical(b_bits, 16)


def _unpack_pair(w):
    a = lax.bitcast_convert_type(w & HI_MASK, F32)
    b = lax.bitcast_convert_type(lax.shift_left(w, 16), F32)
    return a, b


def _ada_kernel(cond_ref, w_ref, b_ref, o_ref):
    s = jax.nn.silu(cond_ref[...])
    o_ref[0] = jnp.dot(s.astype(BF16), w_ref[0].astype(BF16), preferred_element_type=F32) + b_ref[0]


def _ada_modulation(cond, ada_w, ada_b):
    depth, d, n = ada_w.shape
    tn = _tile(n, 1024)
    return pl.pallas_call(
        _ada_kernel,
        grid=(depth, n // tn),
        in_specs=[
            pl.BlockSpec((COND_ROWS, d), lambda l, j: (0, 0)),
            pl.BlockSpec((1, d, tn), lambda l, j: (l, 0, j)),
            pl.BlockSpec((1, 1, tn), lambda l, j: (l, 0, j)),
        ],
        out_specs=pl.BlockSpec((1, COND_ROWS, tn), lambda l, j: (l, 0, j)),
        out_shape=jax.ShapeDtypeStruct((depth, COND_ROWS, n), F32),
        compiler_params=_params(("parallel", "parallel")),
        name="ada_modulation",
    )(cond, ada_w, ada_b.reshape(depth, 1, n))


def _norm_mod(x, gamma, shift, scale):
    y = x * lax.rsqrt(jnp.mean(x * x, axis=-1, keepdims=True) + NORM_EPS)
    return (y * gamma) * (1.0 + scale) + shift


def _mod_row(tile_start, t_ctx, l_lat):
    return jnp.where(tile_start < t_ctx, 0, 1 + jnp.maximum(tile_start - t_ctx, 0) // l_lat)


def _part_specs(parts, tm, index_of, single_buffer=False):
    specs, seg_tiles, lo = [], [], 0
    extra = dict(pipeline_mode=pl.Buffered(1)) if single_buffer and len(parts) > 1 else {}
    for p in parts:
        n_tiles = p.shape[0] // tm
        specs.append(pl.BlockSpec(
            (tm, p.shape[1]),
            lambda *ids, lo=lo, n_tiles=n_tiles: (jnp.clip(index_of(*ids) - lo, 0, n_tiles - 1), 0), **extra))
        seg_tiles.append(n_tiles)
        lo += n_tiles
    return specs, tuple(seg_tiles)


def _for_each_part(i, seg_tiles, fn):
    if len(seg_tiles) == 1:
        fn(0)
        return
    lo = 0
    for s, n_tiles in enumerate(seg_tiles):
        pl.when(jnp.logical_and(i >= lo, i < lo + n_tiles))(functools.partial(fn, s))
        lo += n_tiles


def _norm_matmul_kernel(*refs, shift_idx, scale_idx, act, seg_tiles):
    n_seg = len(seg_tiles)
    x_refs = refs[:n_seg]
    g_ref, mod_ref, w_ref, o_ref, h_ref = refs[n_seg:]

    @pl.when(pl.program_id(1) == 0)
    def _():
        def prologue(s):
            m = mod_ref[0]
            h = _norm_mod(x_refs[s][...], g_ref[...], m[shift_idx:shift_idx + 1], m[scale_idx:scale_idx + 1])
            h_ref[...] = h.astype(BF16)

        _for_each_part(pl.program_id(0), seg_tiles, prologue)

    acc = jnp.dot(h_ref[...], w_ref[0], preferred_element_type=F32)
    if act == "gelu":
        acc = jax.nn.gelu(acc)
    o_ref[...] = acc.astype(o_ref.dtype)


def _norm_matmul(x_parts, gamma, mod, w_stack, w_idx, *, shift_idx, scale_idx, act, t_ctx, l_lat, row_off=0):
    d = x_parts[0].shape[1]
    t = sum(p.shape[0] for p in x_parts)
    n = w_stack.shape[2]
    tm = _tile(min(t_ctx, l_lat), 1024)
    tn = _tile(n, 1024)
    x_specs, seg_tiles = _part_specs(x_parts, tm, lambda i, j: i, single_buffer=True)
    return pl.pallas_call(
        functools.partial(_norm_matmul_kernel, shift_idx=shift_idx, scale_idx=scale_idx, act=act,
                          seg_tiles=seg_tiles),
        grid=(t // tm, n // tn),
        in_specs=x_specs + [
            pl.BlockSpec((1, d), lambda i, j: (0, 0)),
            pl.BlockSpec((1, 6, d), lambda i, j: (_mod_row(row_off + i * tm, t_ctx, l_lat), 0, 0)),
            pl.BlockSpec((1, d, tn), lambda i, j: (w_idx, 0, j)),
        ],
        out_specs=pl.BlockSpec((tm, tn), lambda i, j: (i, j)),
        out_shape=jax.ShapeDtypeStruct((t, n), BF16),
        scratch_shapes=[pltpu.VMEM((tm, d), BF16)],
        compiler_params=_params(("parallel", "arbitrary")),
        name="norm_matmul_" + str(act),
    )(*x_parts, gamma.reshape(1, d), mod, w_stack)


def _retention_kernel(*refs, chunk, n_chunks, hps, use_rope, has_init, emit_state, alias_state, zero_rest, dk):
    it = iter(refs)
    lg_ref, q_ref, k_ref, v_ref, g_ref, gn_ref = (next(it) for _ in range(6))
    cos_ref = sin_ref = s0_ref = st_ref = None
    if use_rope:
        cos_ref, sin_ref = next(it), next(it)
    if has_init:
        s0_ref = next(it)
    if alias_state:
        next(it)
    o_ref = next(it)
    if emit_state:
        st_ref = next(it)
    kr_ref, kf_ref, sb_ref, s_ref = (next(it) for _ in range(4))

    dv = 2 * dk
    c = chunk
    half = dk // 2
    k_scale = dk ** -0.5
    pos = lax.broadcasted_iota(jnp.int32, (c, 1), 0).astype(F32)
    rel = (lax.broadcasted_iota(jnp.int32, (c, c), 0) - lax.broadcasted_iota(jnp.int32, (c, c), 1)).astype(F32)
    tn_dims = (((0,), (0,)), ((), ()))
    nt_dims = (((1,), (1,)), ((), ()))

    def decays(hd):
        head = pl.program_id(1) * hps + hd
        lg_f = lg_ref[0, head]
        lg_b = lg_ref[1, head]
        mask = (jnp.where(rel >= 0, jnp.exp(lg_f * jnp.maximum(rel, 0.0)), 0.0)
                + jnp.where(rel <= 0, jnp.exp(lg_b * jnp.maximum(-rel, 0.0)), 0.0))
        return dict(
            qd_f=jnp.exp(lg_f * (pos + 1.0)), kd_f=jnp.exp(lg_f * (c - 1.0 - pos)) * k_scale,
            qd_b=jnp.exp(lg_b * (c - pos)), kd_b=jnp.exp(lg_b * pos) * k_scale,
            cd_f=jnp.exp(jnp.full((1, 1), c, F32) * lg_f), cd_b=jnp.exp(jnp.full((1, 1), c, F32) * lg_b),
            mask=mask * k_scale)

    dec = [decays(hd) for hd in range(hps)]

    def rows(ci):
        if isinstance(ci, int):
            return pl.ds(ci * c, c)
        return pl.ds(pl.multiple_of(ci * c, c), c)

    def kcols(hd):
        return slice(hd * dk, (hd + 1) * dk)

    def vcols(hd):
        return slice(hd * dv, (hd + 1) * dv)

    def rotated(ref, sl, hd):
        x = ref[0, sl, kcols(hd)].astype(F32)
        if use_rope:
            cos = cos_ref[sl, :]
            sin = sin_ref[sl, :]
            x1, x2 = x[:, :half], x[:, half:]
            x = jnp.concatenate([x1 * cos - x2 * sin, x1 * sin + x2 * cos], axis=-1)
        return x

    def emit(direction, hd, value):
        st_ref[0, 0, direction, hd] = value
        if zero_rest:
            for other in range(1, st_ref.shape[1]):
                st_ref[0, other, direction, hd] = jnp.zeros_like(value)

    def finish(sl, hd, o):
        mu = jnp.mean(o, axis=-1, keepdims=True)
        dlt = o - mu
        var = jnp.mean(dlt * dlt, axis=-1, keepdims=True)
        on = dlt * lax.rsqrt(var + NORM_EPS)
        gate = g_ref[0, sl, vcols(hd)].astype(F32)
        o_ref[0, sl, vcols(hd)] = ((on * gn_ref[hd]) * jax.nn.silu(gate)).astype(o_ref.dtype)

    def scores(q, kb, hd):
        a = lax.dot_general(q.astype(BF16), kb, nt_dims, preferred_element_type=F32) * dec[hd]["mask"]
        return a.astype(BF16)

    if n_chunks == 1 and not has_init:
        sl = rows(0)
        for hd in range(hps):
            q = rotated(q_ref, sl, hd)
            k = rotated(k_ref, sl, hd)
            v = v_ref[0, sl, vcols(hd)]
            finish(sl, hd, jnp.dot(scores(q, k.astype(BF16), hd), v, preferred_element_type=F32))
            if emit_state:
                for direction, kd in ((0, dec[hd]["kd_f"]), (1, dec[hd]["kd_b"])):
                    emit(direction, hd,
                         lax.dot_general((k * kd).astype(BF16), v, tn_dims, preferred_element_type=F32))
        return

    def init_states(direction):
        for hd in range(hps):
            if has_init:
                s_ref[hd] = s0_ref[0, 0, direction, hd]
            else:
                s_ref[hd] = jnp.zeros(s_ref.shape[1:], F32)

    init_states(1)

    def bwd_body(t, carry):
        ci = n_chunks - 1 - t
        sl = rows(ci)
        for hd in range(hps):
            k = rotated(k_ref, sl, hd)
            kr_ref[sl, kcols(hd)] = k.astype(BF16)
            kf_ref[sl, kcols(hd)] = (k * dec[hd]["kd_f"]).astype(BF16)
            sb_ref[ci * hps + hd] = s_ref[hd].astype(BF16)
            kv = lax.dot_general((k * dec[hd]["kd_b"]).astype(BF16), v_ref[0, sl, vcols(hd)], tn_dims,
                                 preferred_element_type=F32)
            s_ref[hd] = dec[hd]["cd_b"] * s_ref[hd] + kv
        return carry

    lax.fori_loop(0, n_chunks, bwd_body, 0, unroll=min(RET_UNROLL, n_chunks))
    if emit_state:
        for hd in range(hps):
            emit(1, hd, s_ref[hd])

    init_states(0)

    def fwd_body(ci, carry):
        sl = rows(ci)
        for hd in range(hps):
            q = rotated(q_ref, sl, hd)
            v = v_ref[0, sl, vcols(hd)]
            o = (jnp.dot(scores(q, kr_ref[sl, kcols(hd)], hd), v, preferred_element_type=F32)
                 + jnp.dot((q * dec[hd]["qd_f"]).astype(BF16), s_ref[hd].astype(BF16),
                           preferred_element_type=F32)
                 + jnp.dot((q * dec[hd]["qd_b"]).astype(BF16), sb_ref[ci * hps + hd],
                           preferred_element_type=F32))
            finish(sl, hd, o)
            kv = lax.dot_general(kf_ref[sl, kcols(hd)], v, tn_dims, preferred_element_type=F32)
            s_ref[hd] = dec[hd]["cd_f"] * s_ref[hd] + kv
        return carry

    lax.fori_loop(0, n_chunks, fwd_body, 0, unroll=min(RET_UNROLL, n_chunks))
    if emit_state:
        for hd in range(hps):
            emit(0, hd, s_ref[hd])


def _retention(qkvg, log_g, gn_g, *, seq_len, n_seq, seq_off, heads_per_step, rope_tables=None,
               init_state=None, state_out=None):
    t, width = qkvg.shape
    h = RET_HEADS
    hps = heads_per_step
    dk = width // (6 * h)
    dv = 2 * dk
    chunk = min(RET_CHUNK_ROWS, seq_len)
    n_chunks = seq_len // chunk
    x3 = qkvg.reshape(t // seq_len, seq_len, width)
    k_blk, v_blk, g_blk = h // hps, (2 * h * dk) // (hps * dv), (2 * h * dk + h * dv) // (hps * dv)
    use_rope = rope_tables is not None
    has_init = init_state is not None
    emit_state = state_out is not None

    in_specs = [
        pl.BlockSpec(memory_space=pltpu.SMEM),
        pl.BlockSpec((1, seq_len, hps * dk), lambda b, hh: (b + seq_off, 0, hh)),
        pl.BlockSpec((1, seq_len, hps * dk), lambda b, hh: (b + seq_off, 0, k_blk + hh)),
        pl.BlockSpec((1, seq_len, hps * dv), lambda b, hh: (b + seq_off, 0, v_blk + hh)),
        pl.BlockSpec((1, seq_len, hps * dv), lambda b, hh: (b + seq_off, 0, g_blk + hh)),
        pl.BlockSpec((hps, 1, dv), lambda b, hh: (hh, 0, 0)),
    ]
    args = [log_g, x3, x3, x3, x3, gn_g.reshape(h, 1, dv)]
    if use_rope:
        in_specs += [pl.BlockSpec((seq_len, dk // 2), lambda b, hh: (0, 0))] * 2
        args += list(rope_tables)
    if has_init:
        layer = init_state[1]
        in_specs.append(pl.BlockSpec((1, 1, 2, hps, dk, dv), lambda b, hh: (b, layer, 0, hh, 0, 0)))
        args.append(init_state[0])
    out_specs = [pl.BlockSpec((1, seq_len, hps * dv), lambda b, hh: (b, 0, hh))]
    out_shape = [jax.ShapeDtypeStruct((n_seq, seq_len, h * dv), BF16)]
    aliases = {}
    alias_state = zero_rest = False
    if emit_state:
        s_layer, n_layers, prev = state_out
        alias_state = prev is not None
        zero_rest = not alias_state and n_layers > 1
        if alias_state:
            aliases = {len(args): 1}
            in_specs.append(pl.BlockSpec(memory_space=pl.ANY))
            args.append(prev)
            out_specs.append(pl.BlockSpec((1, 1, 2, hps, dk, dv), lambda b, hh: (b, s_layer, 0, hh, 0, 0)))
        else:
            assert s_layer == 0
            out_specs.append(pl.BlockSpec((1, n_layers, 2, hps, dk, dv), lambda b, hh: (b, 0, 0, hh, 0, 0)))
        out_shape.append(jax.ShapeDtypeStruct((n_seq, n_layers, 2, h, dk, dv), F32))
    outs = pl.pallas_call(
        functools.partial(_retention_kernel, chunk=chunk, n_chunks=n_chunks, hps=hps, use_rope=use_rope,
                          has_init=has_init, emit_state=emit_state, alias_state=alias_state,
                          zero_rest=zero_rest, dk=dk),
        grid=(n_seq, h // hps),
        in_specs=in_specs,
        out_specs=out_specs,
        out_shape=out_shape,
        scratch_shapes=[pltpu.VMEM((seq_len, hps * dk), BF16), pltpu.VMEM((seq_len, hps * dk), BF16),
                        pltpu.VMEM((n_chunks * hps, dk, dv), BF16), pltpu.VMEM((hps, dk, dv), F32)],
        input_output_aliases=aliases,
        compiler_params=_params(("parallel", "parallel")),
        name="retention_rope" if use_rope else "retention_ctx",
    )(*args)
    return outs if emit_state else outs[0]


def _rope_tables(seq_len, dk):
    rows = seq_len // GRID_W
    row = jnp.broadcast_to(jnp.arange(rows)[:, None], (rows, GRID_W)).reshape(-1).astype(F32)
    col = jnp.broadcast_to(jnp.arange(GRID_W)[None, :], (rows, GRID_W)).reshape(-1).astype(F32)
    n_freq = dk // 4
    inv = ROPE_BASE ** (-jnp.arange(n_freq, dtype=F32) / n_freq)
    ang = jnp.concatenate([row[:, None] * inv, col[:, None] * inv], axis=-1)
    return jnp.cos(ang), jnp.sin(ang)


def _proj_residual_kernel(*refs, gate_idx, seg_tiles, n_x):
    n_seg = len(seg_tiles)
    a_refs = refs[:n_seg]
    x_refs = refs[n_seg:n_seg + n_x]
    w_ref, mod_ref, o_ref = refs[n_seg + n_x:]
    gate = mod_ref[0][gate_idx:gate_idx + 1]

    def seg(s):
        x_ref = x_refs[s] if n_x > 1 else x_refs[0]
        acc = jnp.dot(a_refs[s][...], w_ref[0], preferred_element_type=F32)
        o_ref[...] = x_ref[...] + gate * acc

    _for_each_part(pl.program_id(1), seg_tiles, seg)


def _proj_residual(a_parts, w_stack, w_idx, x_parts, mod, *, gate_idx, t_ctx, l_lat):
    d = x_parts[0].shape[1]
    t = sum(p.shape[0] for p in a_parts)
    k = w_stack.shape[1]
    tm = _tile(min(t_ctx, l_lat), 512)
    tn = _tile(d, 1024)
    a_specs, seg_tiles = _part_specs(a_parts, tm, lambda j, i: i)
    assert sum(seg_tiles) * tm == t
    x_specs = []
    lo = 0
    for p in x_parts:
        n_tiles = p.shape[0] // tm
        x_specs.append(pl.BlockSpec(
            (tm, tn), lambda j, i, lo=lo, n_tiles=n_tiles: (jnp.clip(i - lo, 0, n_tiles - 1), j)))
        lo += n_tiles
    assert len(x_parts) == 1 or [p.shape[0] for p in x_parts] == [p.shape[0] for p in a_parts]
    return pl.pallas_call(
        functools.partial(_proj_residual_kernel, gate_idx=gate_idx, seg_tiles=seg_tiles, n_x=len(x_parts)),
        grid=(d // tn, t // tm),
        in_specs=a_specs + x_specs + [
            pl.BlockSpec((1, k, tn), lambda j, i: (w_idx, 0, j)),
            pl.BlockSpec((1, 6, tn), lambda j, i: (_mod_row(i * tm, t_ctx, l_lat), 0, j)),
        ],
        out_specs=pl.BlockSpec((tm, tn), lambda j, i: (i, j)),
        out_shape=jax.ShapeDtypeStruct((t, d), F32),
        compiler_params=_params(("parallel", "parallel")),
        name="proj_residual",
    )(*a_parts, *x_parts, w_stack, mod)


def _spatial_gate_kernel(u_ref, v_ref, lng_ref, lnb_ref, ws_ref, bs_ref, t_ref, vn_ref, *, n_sub, cw):
    v32 = v_ref[...].astype(F32)
    mu = jnp.mean(v32, axis=-1, keepdims=True)
    dlt = v32 - mu
    var = jnp.mean(dlt * dlt, axis=-1, keepdims=True)
    vn_ref[...] = (dlt * lax.rsqrt(var + NORM_EPS) * lng_ref[...] + lnb_ref[...]).astype(BF16)
    for n in range(n_sub):
        rs = slice(n * CM_CHUNK, (n + 1) * CM_CHUNK)
        for g in range(CM_GROUPS):
            cs = slice(g * cw, (g + 1) * cw)
            s = jnp.dot(ws_ref[g], vn_ref[rs, cs], preferred_element_type=F32) + bs_ref[g]
            t_ref[rs, cs] = (u_ref[rs, cs].astype(F32) * s).astype(t_ref.dtype)


def _spatial_gate(uv, ln_g, ln_b, w_s, b_s):
    t, two_cw = uv.shape
    width = two_cw // 2
    cw = width // CM_GROUPS
    tm = 2 * CM_CHUNK
    return pl.pallas_call(
        functools.partial(_spatial_gate_kernel, n_sub=tm // CM_CHUNK, cw=cw),
        grid=(t // tm,),
        in_specs=[
            pl.BlockSpec((tm, width), lambda i: (i, 0)),
            pl.BlockSpec((tm, width), lambda i: (i, 1)),
            pl.BlockSpec((1, width), lambda i: (0, 0)),
            pl.BlockSpec((1, width), lambda i: (0, 0)),
            pl.BlockSpec((CM_GROUPS, CM_CHUNK, CM_CHUNK), lambda i: (0, 0, 0)),
            pl.BlockSpec((CM_GROUPS, CM_CHUNK, 1), lambda i: (0, 0, 0)),
        ],
        out_specs=pl.BlockSpec((tm, width), lambda i: (i, 0)),
        out_shape=jax.ShapeDtypeStruct((t, width), BF16),
        scratch_shapes=[pltpu.VMEM((tm, width), BF16)],
        compiler_params=_params(("parallel",)),
        name="spatial_gate",
    )(uv, uv, ln_g.reshape(1, width), ln_b.reshape(1, width), w_s.astype(BF16),
      b_s.reshape(CM_GROUPS, CM_CHUNK, 1))


def _router_kernel(x_ref, g_ref, mod_ref, wr_ref, br_ref, h_ref, ids_ref, wts_ref, cnt_ref, run_ref):
    m = mod_ref[0]
    h = _norm_mod(x_ref[...], g_ref[...], m[3:4], m[4:5])
    hp = h.shape[1] // 2
    h_ref[...] = _pack_pair(h[:, :hp], h[:, hp:])
    h_hi = h.astype(BF16)
    h_lo = (h - h_hi.astype(F32)).astype(BF16)
    w = wr_ref[...]
    w_hi = w.astype(BF16)
    w_lo = (w - w_hi.astype(F32)).astype(BF16)
    lg = (jnp.dot(h_hi, w_hi, preferred_element_type=F32)
          + (jnp.dot(h_lo, w_hi, preferred_element_type=F32) + jnp.dot(h_hi, w_lo, preferred_element_type=F32)))
    lg = lg + br_ref[...]

    tm = lg.shape[0]
    lane = lax.broadcasted_iota(I32, lg.shape, 1).astype(F32)
    far = float(ROUTER_LANES)

    def first_max(vals):
        top = jnp.max(vals, axis=-1, keepdims=True)
        return top, jnp.min(jnp.where(vals == top, lane, far), axis=-1, keepdims=True)

    in_groups = lane < N_GROUPS
    g_top, grp = first_max(jnp.where(in_groups, lg, NEG_LOGIT))
    p_grp = 1.0 / jnp.sum(jnp.where(in_groups, jnp.exp(lg - g_top), 0.0), axis=-1, keepdims=True)
    lo = N_GROUPS + EXPERTS_PER_GROUP * grp
    cand = jnp.where(jnp.logical_and(lane >= lo, lane < lo + EXPERTS_PER_GROUP), lg, NEG_LOGIT)
    v1, i1 = first_max(cand)
    v2, i2 = first_max(jnp.where(lane == i1, NEG_LOGIT, cand))
    e2 = jnp.exp(v2 - v1)
    w1 = (1.0 / (1.0 + e2)) * p_grp
    w2 = (e2 / (1.0 + e2)) * p_grp

    @pl.when(pl.program_id(0) == 0)
    def _():
        run_ref[...] = jnp.zeros_like(run_ref)

    oh1 = (lane == i1).astype(F32)
    oh2 = (lane == i2).astype(F32)
    both = oh1 + oh2
    earlier = (lax.broadcasted_iota(I32, (tm, tm), 0) > lax.broadcasted_iota(I32, (tm, tm), 1))
    before = jnp.dot(earlier.astype(BF16), both.astype(BF16), preferred_element_type=F32) + run_ref[0:1, :]
    r1 = jnp.sum(oh1 * before, axis=-1, keepdims=True)
    r2 = jnp.sum(oh2 * before, axis=-1, keepdims=True)
    run_ref[...] = run_ref[...] + jnp.sum(both, axis=0, keepdims=True)
    cnt_ref[...] = run_ref[...].astype(I32)

    ids = jnp.where(lane == 0, i1 - N_GROUPS,
                    jnp.where(lane == 1, i2 - N_GROUPS, jnp.where(lane == 2, r1, jnp.where(lane == 3, r2, 0.0))))
    ids_ref[...] = ids.astype(I32)
    wts_ref[...] = jnp.where(lane == 0, w1, jnp.where(lane == 1, w2, 0.0))


def _router(x, gamma, mod, w_router_all, b_router_all, *, t_ctx, l_lat):
    t, d = x.shape
    tm = _tile(min(t_ctx, l_lat), 512)
    return pl.pallas_call(
        _router_kernel,
        grid=(t // tm,),
        in_specs=[
            pl.BlockSpec((tm, d), lambda i: (i, 0)),
            pl.BlockSpec((1, d), lambda i: (0, 0)),
            pl.BlockSpec((1, 6, d), lambda i: (_mod_row(i * tm, t_ctx, l_lat), 0, 0)),
            pl.BlockSpec((d, ROUTER_LANES), lambda i: (0, 0)),
            pl.BlockSpec((1, ROUTER_LANES), lambda i: (0, 0)),
        ],
        out_specs=[
            pl.BlockSpec((tm, d // 2), lambda i: (i, 0)),
            pl.BlockSpec((tm, ROUTER_LANES), lambda i: (i, 0)),
            pl.BlockSpec((tm, ROUTER_LANES), lambda i: (i, 0)),
            pl.BlockSpec((8, ROUTER_LANES), lambda i: (0, 0)),
        ],
        out_shape=[jax.ShapeDtypeStruct((t, d // 2), I32), jax.ShapeDtypeStruct((t, ROUTER_LANES), I32),
                   jax.ShapeDtypeStruct((t, ROUTER_LANES), F32), jax.ShapeDtypeStruct((8, ROUTER_LANES), I32)],
        scratch_shapes=[pltpu.VMEM((8, ROUTER_LANES), F32)],
        compiler_params=_params(("arbitrary",)),
        name="moe_router",
    )(x, gamma.reshape(1, d), mod, w_router_all, b_router_all)


def _routing_plan(ids, wts, cnt, n_blocks):
    expert_ids = ids[:, 0:TOP_K].reshape(-1)
    rank = ids[:, TOP_K:2 * TOP_K].reshape(-1)
    weights = wts[:, 0:TOP_K]
    counts = cnt[0, N_GROUPS:N_GROUPS + N_EXPERTS]
    e_range = jnp.arange(N_EXPERTS, dtype=I32)
    onehot = (expert_ids[:, None] == e_range[None, :]).astype(I32)
    padded = (counts + MOE_ROWS - 1) // MOE_ROWS * MOE_ROWS
    pad_ends = jnp.cumsum(padded)
    pad_starts = pad_ends - padded
    pos = (jnp.sum(onehot * pad_starts[None, :], axis=1) + rank).astype(I32)
    blk_start = jnp.arange(n_blocks, dtype=I32) * MOE_ROWS
    block_e = jnp.minimum(jnp.sum((pad_ends[None, :] <= blk_start[:, None]).astype(I32), axis=1),
                          N_EXPERTS - 1).astype(I32)
    n_used = (pad_ends[-1] // MOE_ROWS).astype(I32).reshape(1)
    later = jnp.where((counts[None, :] > 0) & (e_range[None, :] > e_range[:, None]), e_range[None, :], N_EXPERTS)
    next_of = jnp.min(later, axis=1)
    next_of = jnp.where(next_of == N_EXPERTS, -1, next_of).astype(I32)
    next_e = jnp.sum((block_e[:, None] == e_range[None, :]).astype(I32) * next_of[None, :], axis=1).astype(I32)
    pad_lo = (pad_starts + counts).astype(I32)
    return weights, pos, block_e, next_e, n_used, pad_lo, pad_ends.astype(I32)


def _dispatch_kernel(pos_ref, lo_ref, hi_ref, h_ref, xb_hbm, zero_ref, sem, zsem):
    i = pl.program_id(0)
    tm = h_ref.shape[0]
    base = i * (tm * TOP_K)

    for r in range(tm):
        for kk in range(TOP_K):
            p = pos_ref[base + (TOP_K * r + kk)]
            pltpu.make_async_copy(h_ref.at[pl.ds(r, 1), :], xb_hbm.at[pl.ds(p, 1), :], sem).start(
                priority=(TOP_K * r + kk) % 2)
    for _ in range(TOP_K):
        pltpu.make_async_copy(h_ref, xb_hbm.at[pl.ds(0, tm), :], sem).wait()

    @pl.when(i == pl.num_programs(0) - 1)
    def _():
        zero_ref[...] = jnp.zeros_like(zero_ref)

        def zero_row(p):
            return pltpu.make_async_copy(zero_ref.at[pl.ds(0, 1), :], xb_hbm.at[pl.ds(p, 1), :], zsem)

        def per_expert(e, carry):
            lo = lo_ref[e]
            hi = hi_ref[e]

            def start(p, c):
                zero_row(p).start()
                return c

            def wait(p, c):
                zero_row(p).wait()
                return c

            lax.fori_loop(lo, hi, start, 0)
            lax.fori_loop(lo, hi, wait, 0)
            return carry

        lax.fori_loop(0, N_EXPERTS, per_expert, 0)

        n_blocks = xb_hbm.shape[0] // MOE_ROWS
        n_used = hi_ref[N_EXPERTS - 1] // MOE_ROWS

        def zero_block(blk):
            dst = xb_hbm.at[pl.ds(pl.multiple_of(blk * MOE_ROWS, MOE_ROWS), MOE_ROWS), :]
            return pltpu.make_async_copy(zero_ref, dst, zsem)

        def start_block(blk, c):
            zero_block(blk).start()
            return c

        def wait_block(blk, c):
            zero_block(blk).wait()
            return c

        lax.fori_loop(n_used, n_blocks, start_block, 0)
        lax.fori_loop(n_used, n_blocks, wait_block, 0)


def _dispatch(hp, pos, pad_lo, pad_hi, n_blocks):
    t, width = hp.shape
    tm = _tile(t, 512)
    grid_spec = pltpu.PrefetchScalarGridSpec(
        num_scalar_prefetch=3,
        grid=(t // tm,),
        in_specs=[pl.BlockSpec((tm, width), lambda i, pos, lo, hi: (i, 0))],
        out_specs=pl.BlockSpec(memory_space=pl.ANY),
        scratch_shapes=[pltpu.VMEM((MOE_ROWS, width), I32), pltpu.SemaphoreType.DMA(()),
                        pltpu.SemaphoreType.DMA(())],
    )
    return pl.pallas_call(
        _dispatch_kernel,
        grid_spec=grid_spec,
        out_shape=jax.ShapeDtypeStruct((n_blocks * MOE_ROWS, width), I32),
        compiler_params=_params(("arbitrary",)),
        name="moe_dispatch",
    )(pos, pad_lo, pad_hi, hp)


def _expert_kernel(be_ref, nx_ref, nu_ref, x_ref, wg_hbm, wu_hbm, wd_hbm, y_ref,
                   sg_ref, su_ref, sd_ref, wg_ref, wu_ref, wd_ref, sem, *, layer):
    i = pl.program_id(0)
    n_used = nu_ref[0]
    e = be_ref[i]
    changed = jnp.logical_or(i == 0, e != be_ref[jnp.maximum(i - 1, 0)])
    hp = x_ref.shape[1]

    def weight_copies(ex):
        return (pltpu.make_async_copy(wg_hbm.at[layer, ex], sg_ref, sem.at[0]),
                pltpu.make_async_copy(wu_hbm.at[layer, ex], su_ref, sem.at[1]),
                pltpu.make_async_copy(wd_hbm.at[layer, ex], sd_ref, sem.at[2]))

    @pl.when(i == 0)
    def _():
        for cp in weight_copies(e):
            cp.start()

    @pl.when(jnp.logical_and(changed, i < n_used))
    def _():
        for cp in weight_copies(e):
            cp.wait()
        for src, dst in ((sg_ref, wg_ref), (su_ref, wu_ref), (sd_ref, wd_ref)):
            rows = src.shape[0]
            step = 256

            def cast(ci, carry, src=src, dst=dst):
                sl = pl.ds(pl.multiple_of(ci * step, step), step)
                dst[sl, :] = src[sl, :].astype(BF16)
                return carry

            lax.fori_loop(0, rows // step, cast, 0)
        nxt = nx_ref[i]

        @pl.when(nxt >= 0)
        def _():
            for cp in weight_copies(nxt):
                cp.start()

    @pl.when(i < n_used)
    def _():
        xa, xb = _unpack_pair(x_ref[...])
        x = jnp.concatenate([xa.astype(BF16), xb.astype(BF16)], axis=1)
        hg = jnp.dot(x, wg_ref[...], preferred_element_type=F32)
        hu = jnp.dot(x, wu_ref[...], preferred_element_type=F32)
        hm = (jax.nn.silu(hg) * hu).astype(BF16)
        y = jnp.dot(hm, wd_ref[...], preferred_element_type=F32)
        y_ref[...] = _pack_pair(y[:, :hp], y[:, hp:])

    @pl.when(i >= n_used)
    def _():
        y_ref[...] = jnp.zeros_like(y_ref)


def _experts(xb, block_e, next_e, n_used, w_gate, w_up, w_down, layer, n_blocks):
    width = xb.shape[1]
    _, _, d, hid = w_gate.shape
    grid_spec = pltpu.PrefetchScalarGridSpec(
        num_scalar_prefetch=3,
        grid=(n_blocks,),
        in_specs=[
            pl.BlockSpec((MOE_ROWS, width), lambda i, be, nx, nu: (jnp.minimum(i, nu[0] - 1), 0)),
            pl.BlockSpec(memory_space=pl.ANY),
            pl.BlockSpec(memory_space=pl.ANY),
            pl.BlockSpec(memory_space=pl.ANY),
        ],
        out_specs=pl.BlockSpec((MOE_ROWS, width), lambda i, be, nx, nu: (i, 0)),
        scratch_shapes=[
            pltpu.VMEM((d, hid), F32), pltpu.VMEM((d, hid), F32), pltpu.VMEM((hid, d), F32),
            pltpu.VMEM((d, hid), BF16), pltpu.VMEM((d, hid), BF16), pltpu.VMEM((hid, d), BF16),
            pltpu.SemaphoreType.DMA((3,)),
        ],
    )
    return pl.pallas_call(
        functools.partial(_expert_kernel, layer=layer),
        grid_spec=grid_spec,
        out_shape=jax.ShapeDtypeStruct((n_blocks * MOE_ROWS, width), I32),
        compiler_params=_params(("arbitrary",)),
        name="moe_experts",
    )(block_e, next_e, n_used, xb, w_gate, w_up, w_down)


def _combine_kernel(pos_ref, x_ref, w_ref, mod_ref, fg_ref, yb_hbm, *rest, final_norm, seg_tiles):
    o_refs = rest[:len(seg_tiles)]
    ybuf, sem = rest[len(seg_tiles):]
    i = pl.program_id(0)
    n_tiles = pl.num_programs(0)
    tm = x_ref.shape[0]
    hp = ybuf.shape[2]

    def start_gather(tile, slot):
        base = tile * (tm * TOP_K)

        for r in range(tm):
            for kk in range(TOP_K):
                p = pos_ref[base + (TOP_K * r + kk)]
                pltpu.make_async_copy(yb_hbm.at[pl.ds(p, 1), :], ybuf.at[slot, pl.ds(kk * tm + r, 1), :],
                                      sem.at[slot]).start(priority=(TOP_K * r + kk) % 2)

    def wait_gather(slot):
        pltpu.make_async_copy(yb_hbm.at[pl.ds(0, TOP_K * tm), :], ybuf.at[slot], sem.at[slot]).wait()

    slot = lax.rem(i, 2)

    @pl.when(i == 0)
    def _():
        start_gather(0, 0)

    @pl.when(i + 1 < n_tiles)
    def _():
        start_gather(i + 1, 1 - slot)

    wait_gather(slot)
    w = w_ref[...]
    a0, b0 = _unpack_pair(ybuf[slot, 0:tm, :])
    a1, b1 = _unpack_pair(ybuf[slot, tm:2 * tm, :])
    gate = mod_ref[0][5:6]
    out_a = x_ref[:, :hp] + gate[:, :hp] * (w[:, 0:1] * a0 + w[:, 1:2] * a1)
    out_b = x_ref[:, hp:] + gate[:, hp:] * (w[:, 0:1] * b0 + w[:, 1:2] * b1)
    if final_norm:
        ms = (jnp.sum(out_a * out_a, axis=-1, keepdims=True)
              + jnp.sum(out_b * out_b, axis=-1, keepdims=True)) / (2 * hp)
        inv = lax.rsqrt(ms + NORM_EPS)
        out_a = out_a * inv * fg_ref[:, :hp]
        out_b = out_b * inv * fg_ref[:, hp:]

    def store(s):
        o_refs[s][:, :hp] = out_a
        o_refs[s][:, hp:] = out_b

    _for_each_part(i, seg_tiles, store)


def _combine(x, yb, pos, weights, mod, final_g, *, final_norm, out_rows, t_ctx, l_lat):
    t, d = x.shape
    tm = _tile(min(t_ctx, l_lat), 256)
    seg_tiles = tuple(r // tm for r in out_rows)
    assert sum(seg_tiles) * tm == t
    out_specs, lo = [], 0
    for n_seg in seg_tiles:
        out_specs.append(pl.BlockSpec(
            (tm, d), lambda i, pos, lo=lo, n_seg=n_seg: (jnp.clip(i - lo, 0, n_seg - 1), 0)))
        lo += n_seg
    grid_spec = pltpu.PrefetchScalarGridSpec(
        num_scalar_prefetch=1,
        grid=(t // tm,),
        in_specs=[
            pl.BlockSpec((tm, d), lambda i, pos: (i, 0)),
            pl.BlockSpec((tm, TOP_K), lambda i, pos: (i, 0)),
            pl.BlockSpec((1, 6, d), lambda i, pos: (_mod_row(i * tm, t_ctx, l_lat), 0, 0)),
            pl.BlockSpec((1, d), lambda i, pos: (0, 0)),
            pl.BlockSpec(memory_space=pl.ANY),
        ],
        out_specs=out_specs,
        scratch_shapes=[pltpu.VMEM((2, TOP_K * tm, d // 2), I32), pltpu.SemaphoreType.DMA((2,))],
    )
    return pl.pallas_call(
        functools.partial(_combine_kernel, final_norm=final_norm, seg_tiles=seg_tiles),
        grid_spec=grid_spec,
        out_shape=[jax.ShapeDtypeStruct((r, d), F32) for r in out_rows],
        compiler_params=_params(("arbitrary",)),
        name="moe_combine",
    )(pos, x, weights, mod, final_g.reshape(1, d), yb)


def _moe(x, gamma, mod, layer, w_group, b_group, w_router, b_router, w_gate, w_up, w_down, final_g, *,
         final_norm, out_rows, t_ctx, l_lat):
    t, d = x.shape
    pad = ROUTER_LANES - N_GROUPS - N_EXPERTS
    w_all = jnp.concatenate([w_group[layer], w_router[layer], jnp.zeros((d, pad), F32)], axis=1)
    b_all = jnp.concatenate([b_group[layer], b_router[layer], jnp.zeros((pad,), F32)]).reshape(1, ROUTER_LANES)
    hp, ids, wts, cnt = _router(x, gamma, mod, w_all, b_all, t_ctx=t_ctx, l_lat=l_lat)
    n_blocks = (t * TOP_K) // MOE_ROWS + N_EXPERTS
    weights, pos, block_e, next_e, n_used, pad_lo, pad_hi = _routing_plan(ids, wts, cnt, n_blocks)
    xb = _dispatch(hp, pos, pad_lo, pad_hi, n_blocks)
    yb = _experts(xb, block_e, next_e, n_used, w_gate, w_up, w_down, layer, n_blocks)
    return _combine(x, yb, pos, weights, mod, final_g, final_norm=final_norm, out_rows=out_rows,
                    t_ctx=t_ctx, l_lat=l_lat)


def kernel(x_prompt, x_sample, c, state_ret, c_ctx, ada_w, ada_b, norm1_g, norm2_g, ret_w_qkvg, ret_w_o,
           ret_gn_g, ret_log_decay, cm_w_in, cm_ln_g, cm_ln_b, cm_w_s, cm_b_s, cm_w_out, moe_w_group,
           moe_b_group, moe_w_router, moe_b_router, moe_w_gate, moe_w_up, moe_w_down, final_norm_g):
    batch, seq, d = x_prompt.shape
    dec_batch, dec_seq, _ = x_sample.shape
    depth = ada_w.shape[0]
    n_ret = ret_w_qkvg.shape[0]
    t_ctx = batch * seq
    assert 1 + dec_batch <= COND_ROWS
    tiles = dict(t_ctx=t_ctx, l_lat=dec_seq)

    t_lat = dec_batch * dec_seq
    x_parts = [x_prompt.reshape(t_ctx, d), x_sample.reshape(t_lat, d)]
    cond = jnp.concatenate([c_ctx[None, :], c, jnp.zeros((COND_ROWS - 1 - dec_batch, d), F32)], axis=0)
    mod_all = _ada_modulation(cond, ada_w, ada_b).reshape(depth, COND_ROWS, 6, d)
    dk = d // RET_HEADS
    rope = _rope_tables(dec_seq, dk)
    w_qkvg, w_o = ret_w_qkvg.astype(BF16), ret_w_o.astype(BF16)
    w_in, w_out = cm_w_in.astype(BF16), cm_w_out.astype(BF16)

    states = None
    for layer in range(depth):
        mod = mod_all[layer]
        j = layer // 2
        last = layer == depth - 1
        if layer % 2 == 0:
            project = functools.partial(_norm_matmul, gamma=norm1_g[layer], mod=mod, w_stack=w_qkvg, w_idx=j,
                                        shift_idx=0, scale_idx=1, act=None, **tiles)
            if len(x_parts) == 2:
                qkvg_ctx, qkvg_lat, lat_off = project([x_parts[0]]), project([x_parts[1]], row_off=t_ctx), 0
            else:
                qkvg_ctx = qkvg_lat = project(x_parts)
                lat_off = t_ctx // dec_seq
            log_g = -jnp.exp(ret_log_decay[j].astype(F32))
            o_ctx, states = _retention(qkvg_ctx, log_g, ret_gn_g[j], seq_len=seq, n_seq=batch, seq_off=0,
                                       heads_per_step=2, state_out=(j, n_ret, states))
            o_lat = _retention(qkvg_lat, log_g, ret_gn_g[j], seq_len=dec_seq, n_seq=dec_batch,
                               seq_off=lat_off, heads_per_step=1, rope_tables=rope, init_state=(state_ret, j))
            a_parts = [o_ctx.reshape(t_ctx, -1), o_lat.reshape(t_lat, -1)]
            x = _proj_residual(a_parts, w_o, j, x_parts, mod, gate_idx=2, **tiles)
        else:
            uv = _norm_matmul(x_parts, norm1_g[layer], mod, w_in, j, shift_idx=0, scale_idx=1, act="gelu",
                              **tiles)
            gated = _spatial_gate(uv, cm_ln_g[j], cm_ln_b[j], cm_w_s[j], cm_b_s[j])
            x = _proj_residual([gated], w_out, j, x_parts, mod, gate_idx=2, **tiles)
        x_parts = _moe(x, norm2_g[layer], mod, layer, moe_w_group, moe_b_group, moe_w_router, moe_b_router,
                       moe_w_gate, moe_w_up, moe_w_down, final_norm_g, final_norm=last,
                       out_rows=(t_ctx, t_lat) if last else (t_ctx + t_lat,), **tiles)

    y_prompt = x_parts[0].reshape(batch, seq, d)
    y_sample = x_parts[1].reshape(dec_batch, dec_seq, d)
    return (y_prompt, y_sample, states)
```
